```python
import jax
import jax.numpy as jnp
from jax import lax
import numpy as np

D_MODEL = 1024
BATCH = 16
SEQ = 2048
DEPTH = 1

MIX_WIDTH = D_MODEL
N_ATTN_HEADS = 8
ATTN_HEAD_DIM = 64
N_KV_HEADS = 2
N_IDX_HEADS = 4
IDX_HEAD_DIM = 64
TOPK_KEYS = 256
Q_BLOCK = 128
N_GLA_HEADS = 4
GLA_KEY_DIM = 64
GLA_VAL_DIM = 128
GLA_GATE_RANK = 16
GLA_GATE_TAU = 16.0
GLA_CHUNK = 64
ROPE_THETA = 500000.0
ROPE_FRACTION = 4
N_EXPERTS = 256
N_EXPERT_GROUPS = 8
TOPK_GROUPS = 4
TOPK_EXPERTS = 8
EXPERT_DIM = 256
ROUTED_SCALE = 2.5
EXPERT_BLOCK = 128
NORM_EPS = 1e-6
IN_SIZES = (
    N_ATTN_HEADS * ATTN_HEAD_DIM,
    N_KV_HEADS * ATTN_HEAD_DIM,
    N_KV_HEADS * ATTN_HEAD_DIM,
    N_IDX_HEADS * IDX_HEAD_DIM,
    IDX_HEAD_DIM,
    N_IDX_HEADS,
    N_GLA_HEADS * GLA_KEY_DIM,
    N_GLA_HEADS * GLA_KEY_DIM,
    N_GLA_HEADS * GLA_VAL_DIM,
    GLA_GATE_RANK,
    N_GLA_HEADS * GLA_VAL_DIM,
)
IN_WIDTH = sum(IN_SIZES)

kernel_name = 'hybrid_dsa_gla_moe_block'


def rms_norm(x, w):
    xf = x.astype(jnp.float32)
    y = xf * lax.rsqrt(jnp.mean(xf * xf, axis=-1, keepdims=True) + NORM_EPS)
    return (y * w.astype(jnp.float32)).astype(x.dtype)


def partial_rope(x, positions):
    d = x.shape[-1]
    rot = d // ROPE_FRACTION
    half = rot // 2
    inv_freq = jnp.power(jnp.float32(ROPE_THETA), -jnp.arange(half, dtype=jnp.float32) / half)
    ang = positions.astype(jnp.float32)[:, :, None, None] * inv_freq
    cos, sin = jnp.cos(ang), jnp.sin(ang)
    xf = x.astype(jnp.float32)
    x1, x2, xp = xf[..., :half], xf[..., half:rot], xf[..., rot:]
    out = jnp.concatenate([x1 * cos - x2 * sin, x2 * cos + x1 * sin, xp], axis=-1)
    return out.astype(x.dtype)


def dsa_attention(q, k, v, q_idx, k_idx, w_idx):
    B, S, H, dh = q.shape
    n_sel = min(TOPK_KEYS, S // 4)
    n_blk = S // Q_BLOCK
    rep = H // N_KV_HEADS
    key_pos = jnp.arange(S, dtype=jnp.int32)
    gather_rows = jax.vmap(lambda a, i: a[i])

    def to_blocks(a):
        return jnp.swapaxes(a.reshape((B, n_blk, Q_BLOCK) + a.shape[2:]), 0, 1)

    def block(args):
        qb, qib, wib, blk = args
        q_pos = blk * Q_BLOCK + jnp.arange(Q_BLOCK, dtype=jnp.int32)
        dots = jnp.einsum('bqhd,bsd->bqhs', qib, k_idx, preferred_element_type=jnp.float32)
        score = jnp.einsum('bqh,bqhs->bqs', wib.astype(jnp.float32), jax.nn.relu(dots))
        score = jnp.where((key_pos[None, :] <= q_pos[:, None])[None], score, -jnp.inf)
        _, sel = lax.top_k(score, n_sel)
        valid = sel <= q_pos[None, :, None]
        k_sel = gather_rows(k, sel)
        v_sel = gather_rows(v, sel)
        qg = qb.reshape(B, Q_BLOCK, N_KV_HEADS, rep, dh)
        logits = jnp.einsum('bqgrd,bqkgd->bqgrk', qg, k_sel,
                            preferred_element_type=jnp.float32) * (dh ** -0.5)
        logits = jnp.where(valid[:, :, None, None, :], logits, -jnp.inf)
        p = jax.nn.softmax(logits, axis=-1).astype(v.dtype)
        o = jnp.einsum('bqgrk,bqkgd->bqgrd', p, v_sel)
        return o.reshape(B, Q_BLOCK, H, dh)

    out = lax.map(block, (to_blocks(q), to_blocks(q_idx), to_blocks(w_idx),
                          jnp.arange(n_blk, dtype=jnp.int32)))
    return jnp.swapaxes(out, 0, 1).reshape(B, S, H, dh)


def gla_chunked(q, k, v, log_a):
    B, S, H, dk = q.shape
    dv = v.shape[-1]
    n = S // GLA_CHUNK

    def chunks(a):
        return a.astype(jnp.float32).reshape(B, n, GLA_CHUNK, H, a.shape[-1]).transpose(1, 0, 3, 2, 4)

    qc = chunks(q) * (dk ** -0.5)
    kc, vc, gc = chunks(k), chunks(v), chunks(log_a)
    tri = jnp.tril(jnp.ones((GLA_CHUNK, GLA_CHUNK), dtype=bool))[:, :, None]

    def step(state, inp):
        qb, kb, vb, gb = inp
        b = jnp.cumsum(gb, axis=2)
        diff = jnp.where(tri, b[:, :, :, None, :] - b[:, :, None, :, :], -jnp.inf)
        a = jnp.einsum('bhtsd,bhsd->bhts', qb[:, :, :, None, :] * jnp.exp(diff), kb)
        intra = jnp.einsum('bhts,bhsv->bhtv', a, vb)
        inter = jnp.einsum('bhtd,bhdv->bhtv', qb * jnp.exp(b), state)
        b_end = b[:, :, -1, :]
        new_state = state * jnp.exp(b_end)[..., None] + jnp.einsum(
            'bhsd,bhsv->bhdv', kb * jnp.exp(b_end[:, :, None, :] - b), vb)
        return new_state, intra + inter

    _, o = lax.scan(step, jnp.zeros((B, H, dk, dv), jnp.float32), (qc, kc, vc, gc))
    return o.transpose(1, 0, 3, 2, 4).reshape(B, S, H, dv)


def moe_ffn(h, w_router, b_router, we_gate, we_up, we_down, ws_gate, ws_up, ws_down):
    B, S, D = h.shape
    T = B * S
    xt = h.reshape(T, D)
    per_group = N_EXPERTS // N_EXPERT_GROUPS
    scores = jax.nn.sigmoid(jnp.einsum('td,de->te', xt, w_router, preferred_element_type=jnp.float32))
    biased = scores + b_router.astype(jnp.float32)
    group_score = lax.top_k(biased.reshape(T, N_EXPERT_GROUPS, per_group), 2)[0].sum(-1)
    _, top_groups = lax.top_k(group_score, TOPK_GROUPS)
    group_mask = jnp.any(top_groups[:, :, None] == jnp.arange(N_EXPERT_GROUPS)[None, None, :], axis=1)
    masked = jnp.where(jnp.repeat(group_mask, per_group, axis=1), biased, -jnp.inf)
    _, sel = lax.top_k(masked, TOPK_EXPERTS)
    w = jnp.take_along_axis(scores, sel, axis=-1)
    w = w / jnp.sum(w, axis=-1, keepdims=True) * ROUTED_SCALE
    M = EXPERT_BLOCK
    A = T * TOPK_EXPERTS
    e_flat = sel.reshape(A)
    tok_flat = jnp.repeat(jnp.arange(T, dtype=jnp.int32), TOPK_EXPERTS)
    w_flat = w.reshape(A)
    order = jnp.argsort(e_flat)
    e_sorted, tok_sorted, w_sorted = e_flat[order], tok_flat[order], w_flat[order]
    counts = jnp.zeros((N_EXPERTS,), jnp.int32).at[e_flat].add(1)
    padded = (counts + M - 1) // M * M
    pad_end = jnp.cumsum(padded)
    pad_start = pad_end - padded
    start = jnp.cumsum(counts) - counts
    dest = pad_start[e_sorted] + jnp.arange(A, dtype=jnp.int32) - start[e_sorted]
    n_rows = (A + M - 1) // M * M + N_EXPERTS * M
    n_blocks = n_rows // M
    buf_tok = jnp.full((n_rows,), T, jnp.int32).at[dest].set(tok_sorted)
    buf_w = jnp.zeros((n_rows,), jnp.float32).at[dest].set(w_sorted)
    blk_expert = jnp.minimum(
        jnp.searchsorted(pad_end, jnp.arange(n_blocks, dtype=jnp.int32) * M, side='right'),
        N_EXPERTS - 1)
    x_pad = jnp.concatenate([xt, jnp.zeros((1, D), xt.dtype)], axis=0)
    xb = x_pad[buf_tok].reshape(n_blocks, M, D)

    def expert_block(args):
        xblk, e = args
        return (jax.nn.silu(xblk @ we_gate[e]) * (xblk @ we_up[e])) @ we_down[e]

    yb = lax.map(expert_block, (xb, blk_expert)).reshape(n_rows, D)
    routed = jax.ops.segment_sum(yb * buf_w[:, None].astype(yb.dtype), buf_tok,
                                 num_segments=T + 1)[:T]
    shared = (jax.nn.silu(xt @ ws_gate) * (xt @ ws_up)) @ ws_down
    return (routed + shared).reshape(B, S, D)


def setup_inputs(seed: int = 0) -> dict:
    key = jax.random.key(seed)
    ks = jax.random.split(key, 24)
    f32 = jnp.float32
    L, D = DEPTH, D_MODEL

    def nrm(k, shape, scale):
        return jax.random.normal(k, shape, f32) * scale

    def gain(k, shape):
        return 1.0 + 0.02 * jax.random.normal(k, shape, f32)

    x = nrm(ks[0], (BATCH, SEQ, D), 1.0)
    c = nrm(ks[1], (BATCH, D), 1.0)
    offsets = jax.random.randint(ks[2], (BATCH, 1), 0, 4096, dtype=jnp.int32)
    positions = offsets + jnp.arange(SEQ, dtype=jnp.int32)[None, :]
    return {
        'x': x,
        'c': c,
        'positions': positions,
        'norm1_w': gain(ks[3], (L, D)),
        'norm2_w': gain(ks[4], (L, D)),
        'w_ada': nrm(ks[5], (L, D, 6 * D), 0.5 * D ** -0.5),
        'b_ada': nrm(ks[6], (L, 6 * D), 0.01),
        'w_in': nrm(ks[7], (L, D, IN_WIDTH), D ** -0.5),
        'q_norm_w': gain(ks[8], (L, ATTN_HEAD_DIM)),
        'k_norm_w': gain(ks[9], (L, ATTN_HEAD_DIM)),
        'attn_out_norm_w': gain(ks[10], (L, N_ATTN_HEADS, ATTN_HEAD_DIM)),
        'w_gk2': nrm(ks[11], (L, GLA_GATE_RANK, N_GLA_HEADS * GLA_KEY_DIM), GLA_GATE_RANK ** -0.5),
        'b_gk': nrm(ks[12], (L, N_GLA_HEADS * GLA_KEY_DIM), 0.1),
        'gla_norm_w': gain(ks[13], (L, N_GLA_HEADS, GLA_VAL_DIM)),
        'w_out': nrm(ks[14], (L, MIX_WIDTH, D), MIX_WIDTH ** -0.5),
        'w_router': nrm(ks[15], (L, D, N_EXPERTS), D ** -0.5),
        'b_router': nrm(ks[16], (L, N_EXPERTS), 0.01),
        'we_gate': nrm(ks[17], (L, N_EXPERTS, D, EXPERT_DIM), D ** -0.5),
        'we_up': nrm(ks[18], (L, N_EXPERTS, D, EXPERT_DIM), D ** -0.5),
        'we_down': nrm(ks[19], (L, N_EXPERTS, EXPERT_DIM, D), EXPERT_DIM ** -0.5),
        'ws_gate': nrm(ks[20], (L, D, EXPERT_DIM), D ** -0.5),
        'ws_up': nrm(ks[21], (L, D, EXPERT_DIM), D ** -0.5),
        'ws_down': nrm(ks[22], (L, EXPERT_DIM, D), EXPERT_DIM ** -0.5),
    }


def reference(x, c, positions, norm1_w, norm2_w, w_ada, b_ada, w_in, q_norm_w, k_norm_w,
              attn_out_norm_w, w_gk2, b_gk, gla_norm_w, w_out, w_router, b_router,
              we_gate, we_up, we_down, ws_gate, ws_up, ws_down):
    B, S, D = x.shape
    split_at = np.cumsum(IN_SIZES)[:-1].tolist()
    cond = jax.nn.silu(c.astype(jnp.float32))
    for l in range(DEPTH):
        mod = (cond @ w_ada[l].astype(jnp.float32) + b_ada[l].astype(jnp.float32)).astype(x.dtype)
        sh1, sc1, g1, sh2, sc2, g2 = [m[:, None, :] for m in jnp.split(mod, 6, axis=-1)]
        h = rms_norm(x, norm1_w[l]) * (1 + sc1) + sh1
        proj = h @ w_in[l]
        qa, ka, va, qi, ki, wi, qg, kg, vg, gk, go = jnp.split(proj, split_at, axis=-1)
        qa = partial_rope(rms_norm(qa.reshape(B, S, N_ATTN_HEADS, ATTN_HEAD_DIM), q_norm_w[l]), positions)
        ka = partial_rope(rms_norm(ka.reshape(B, S, N_KV_HEADS, ATTN_HEAD_DIM), k_norm_w[l]), positions)
        va = va.reshape(B, S, N_KV_HEADS, ATTN_HEAD_DIM)
        qi = partial_rope(qi.reshape(B, S, N_IDX_HEADS, IDX_HEAD_DIM), positions)
        ki = partial_rope(ki.reshape(B, S, 1, IDX_HEAD_DIM), positions)[:, :, 0, :]
        wi = wi * ((N_IDX_HEADS * IDX_HEAD_DIM) ** -0.5)
        attn = dsa_attention(qa, ka, va, qi, ki, wi)
        attn = rms_norm(attn, attn_out_norm_w[l]).reshape(B, S, N_ATTN_HEADS * ATTN_HEAD_DIM)
        log_a = jax.nn.log_sigmoid((gk @ w_gk2[l] + b_gk[l]).astype(jnp.float32)) / GLA_GATE_TAU
        o = gla_chunked(qg.reshape(B, S, N_GLA_HEADS, GLA_KEY_DIM),
                        kg.reshape(B, S, N_GLA_HEADS, GLA_KEY_DIM),
                        vg.reshape(B, S, N_GLA_HEADS, GLA_VAL_DIM),
                        log_a.reshape(B, S, N_GLA_HEADS, GLA_KEY_DIM))
        gla = rms_norm(o.astype(x.dtype), gla_norm_w[l]).reshape(B, S, N_GLA_HEADS * GLA_VAL_DIM)
        gla = gla * jax.nn.silu(go)
        mixed = jnp.concatenate([attn, gla], axis=-1) @ w_out[l]
        x = x + g1 * mixed
        h2 = rms_norm(x, norm2_w[l]) * (1 + sc2) + sh2
        x = x + g2 * moe_ffn(h2, w_router[l], b_router[l], we_gate[l], we_up[l], we_down[l],
                             ws_gate[l], ws_up[l], ws_down[l])
    return x
```

```python
import functools

import numpy as np
import jax
import jax.numpy as jnp
from jax import lax
from jax.experimental import pallas as pl
from jax.experimental.pallas import tpu as pltpu

F32 = jnp.float32
BF16 = jnp.bfloat16
I32 = jnp.int32

LANE = 128
SUBLANE = 8
VMEM_LIMIT = 48 * 1024 * 1024

N_ATTN_HEADS = 8
ATTN_HEAD_DIM = 64
N_KV_HEADS = 2
N_IDX_HEADS = 4
IDX_HEAD_DIM = 64
TOPK_KEYS = 256
N_GLA_HEADS = 4
GLA_KEY_DIM = 64
GLA_VAL_DIM = 128
GLA_GATE_RANK = 16
GLA_GATE_TAU = 16.0
GLA_CHUNK = 64
ROPE_THETA = 500000.0
ROPE_FRACTION = 4
N_EXPERTS = 256
N_EXPERT_GROUPS = 8
TOPK_GROUPS = 4
TOPK_EXPERTS = 8
EXPERT_DIM = 256
ROUTED_SCALE = 2.5
NORM_EPS = 1e-6
IN_SIZES = (512, 128, 128, 256, 64, 4, 256, 256, 512, 16, 512)

ROW_SLABS = 8
EXPERT_ROWS = 256
NT_DIMS = (((1,), (1,)), ((), ()))
TN_DIMS = (((0,), (0,)), ((), ()))


def _cparams(*sem):
    return pltpu.CompilerParams(dimension_semantics=sem, vmem_limit_bytes=VMEM_LIMIT)


def _silu(v):
    return v * jax.nn.sigmoid(v)


def _ada_kernel(c_ref, w_ref, b_ref, o_ref):
    cond = _silu(c_ref[...])
    o_ref[...] = jnp.dot(cond.astype(BF16), w_ref[...].astype(BF16),
                         preferred_element_type=F32) + b_ref[...]


def _ada(c, w, b):
    bsz, d = c.shape
    n = w.shape[1]
    tn = 1536
    return pl.pallas_call(
        _ada_kernel,
        grid=(n // tn,),
        in_specs=[pl.BlockSpec((bsz, d), lambda i: (0, 0)),
                  pl.BlockSpec((d, tn), lambda i: (0, i)),
                  pl.BlockSpec((1, tn), lambda i: (0, i))],
        out_specs=pl.BlockSpec((bsz, tn), lambda i: (0, i)),
        out_shape=jax.ShapeDtypeStruct((bsz, n), F32),
        compiler_params=_cparams("arbitrary"),
        name="ada",
    )(c, w, b.reshape(1, n))


def _rope_kernel(pos_ref, invf_ref, m1_ref, m2_ref, cos_ref, s1_ref, s2_ref):
    ang = pos_ref[...] * invf_ref[...]
    s = jnp.sin(ang)
    cos_ref[...] = jnp.cos(ang)
    s1_ref[...] = -s * m1_ref[...]
    s2_ref[...] = s * m2_ref[...]


def _rope_tables(positions):
    t = positions.size
    rot = ATTN_HEAD_DIM // ROPE_FRACTION
    half = rot // 2
    inv_freq = jnp.power(jnp.float32(ROPE_THETA), -jnp.arange(half, dtype=F32) / half)
    lane = np.arange(LANE) % ATTN_HEAD_DIM
    invf = jnp.where(lane < rot, inv_freq[lane % half], 0.0).astype(F32).reshape(1, LANE)
    m1 = jnp.asarray((lane < half).astype(np.float32)).reshape(1, LANE)
    m2 = jnp.asarray(((lane >= half) & (lane < rot)).astype(np.float32)).reshape(1, LANE)
    pos = jnp.broadcast_to(positions.reshape(t, 1).astype(F32), (t, LANE))
    tm = min(2048, t)
    row = pl.BlockSpec((tm, LANE), lambda i: (i, 0))
    one = pl.BlockSpec((1, LANE), lambda i: (0, 0))
    sds = jax.ShapeDtypeStruct((t, LANE), F32)
    return pl.pallas_call(
        _rope_kernel,
        grid=(t // tm,),
        in_specs=[row, one, one, one],
        out_specs=[row, row, row],
        out_shape=[sds, sds, sds],
        compiler_params=_cparams("arbitrary"),
        name="rope",
    )(pos, invf, m1, m2)


_HP = LANE
W_QA = N_ATTN_HEADS * _HP
W_KA = N_KV_HEADS * _HP
W_QI = N_IDX_HEADS * _HP
W_GQ = N_GLA_HEADS * GLA_KEY_DIM
W_GV = N_GLA_HEADS * GLA_VAL_DIM
_GROUPS = (("qa", W_QA), ("kv", 2 * W_KA), ("qi", W_QI), ("kw", 2 * _HP),
           ("gqk", 2 * W_GQ), ("gv", W_GV), ("go", W_GV), ("gk", _HP))
_OFF = {}
_o = 0
for _n, _w in _GROUPS:
    _OFF[_n] = (_o, _o + _w)
    _o += _w
W_IN_PACKED = _o


def _pad_heads(w, n_heads, hd):
    d = w.shape[0]
    w = w.reshape(d, n_heads, hd)
    return jnp.pad(w, ((0, 0), (0, 0), (0, _HP - hd))).reshape(d, n_heads * _HP)


def _pack_w_in(w_in):
    ends = np.cumsum(IN_SIZES)
    starts = ends - np.asarray(IN_SIZES)
    qa, ka, va, qi, ki, wi, qg, kg, vg, gk, go = [w_in[:, a:b] for a, b in zip(starts, ends)]
    d = w_in.shape[0]
    pad_to = lambda w: jnp.pad(w, ((0, 0), (0, _HP - w.shape[1])))
    cols = [_pad_heads(qa, N_ATTN_HEADS, ATTN_HEAD_DIM),
            _pad_heads(ka, N_KV_HEADS, ATTN_HEAD_DIM), _pad_heads(va, N_KV_HEADS, ATTN_HEAD_DIM),
            _pad_heads(qi, N_IDX_HEADS, IDX_HEAD_DIM),
            _pad_heads(ki, 1, IDX_HEAD_DIM), pad_to(wi),
            qg, kg, vg, go, pad_to(gk)]
    w = jnp.concatenate(cols, axis=1).astype(BF16)
    assert w.shape == (d, W_IN_PACKED)
    return w


def _inproj_kernel(x_ref, sc_ref, sh_ref, n1_ref, w_ref, cos_ref, s1_ref, s2_ref, qn_ref, kn_ref,
                   qa_ref, ka_ref, va_ref, qi_ref, ki_ref, wi_ref,
                   qg_ref, kg_ref, vg_ref, gk_ref, go_ref):
    x = x_ref[...]
    ms = jnp.mean(x * x, axis=-1, keepdims=True)
    h = x * lax.rsqrt(ms + NORM_EPS) * n1_ref[...]
    h = h * (1.0 + sc_ref[0]) + sh_ref[0]
    hb = h.astype(BF16)
    cos = cos_ref[...]
    s1 = s1_ref[...]
    s2 = s2_ref[...]

    def proj(name):
        a, b = _OFF[name]
        return jnp.dot(hb, w_ref[:, a:b], preferred_element_type=F32)

    def rope(v):
        return v * cos + pltpu.roll(v, LANE - 8, 1) * s1 + pltpu.roll(v, 8, 1) * s2

    def head_norm(v, w):
        ssq = jnp.sum(v * v, axis=-1, keepdims=True)
        return v * lax.rsqrt(ssq * (1.0 / ATTN_HEAD_DIM) + NORM_EPS) * w

    p = proj("qa")
    for hh in range(N_ATTN_HEADS):
        sl = slice(hh * _HP, (hh + 1) * _HP)
        qa_ref[:, sl] = (rope(head_norm(p[:, sl], qn_ref[...])) * (ATTN_HEAD_DIM ** -0.5)).astype(BF16)
    p = proj("kv")
    for hh in range(N_KV_HEADS):
        sl = slice(hh * _HP, (hh + 1) * _HP)
        ka_ref[:, sl] = rope(head_norm(p[:, sl], kn_ref[...])).astype(BF16)
    va_ref[...] = p[:, W_KA:].astype(BF16)
    p = proj("qi")
    for hh in range(N_IDX_HEADS):
        sl = slice(hh * _HP, (hh + 1) * _HP)
        qi_ref[:, sl] = rope(p[:, sl]).astype(BF16)
    p = proj("kw")
    ki_ref[...] = rope(p[:, :_HP]).astype(BF16)
    wi_ref[...] = p[:, _HP:] * ((N_IDX_HEADS * IDX_HEAD_DIM) ** -0.5)
    p = proj("gqk")
    qg_ref[...] = p[:, :W_GQ]
    kg_ref[...] = p[:, W_GQ:]
    vg_ref[...] = proj("gv")
    go_ref[...] = proj("go")
    gk_ref[...] = proj("gk")


def _inproj(x2, sc1, sh1, n1w, w_packed, cos, s1, s2, qnw, knw, seq):
    t, d = x2.shape
    tm = min(512, seq)
    per_b = seq // tm
    row = lambda w: pl.BlockSpec((tm, w), lambda i: (i, 0))
    mod = pl.BlockSpec((1, 1, d), lambda i: (i // per_b, 0, 0))
    one = lambda w: pl.BlockSpec((1, w), lambda i: (0, 0))
    widths = (W_QA, W_KA, W_KA, W_QI, _HP, _HP, W_GQ, W_GQ, W_GV, _HP, W_GV)
    dtypes = (BF16, BF16, BF16, BF16, BF16, F32, F32, F32, F32, F32, F32)
    return pl.pallas_call(
        _inproj_kernel,
        grid=(t // tm,),
        in_specs=[row(d), mod, mod, one(d),
                  pl.BlockSpec((d, W_IN_PACKED), lambda i: (0, 0)),
                  row(LANE), row(LANE), row(LANE), one(LANE), one(LANE)],
        out_specs=[row(w) for w in widths],
        out_shape=[jax.ShapeDtypeStruct((t, w), dt) for w, dt in zip(widths, dtypes)],
        compiler_params=_cparams("arbitrary"),
        name="inproj",
    )(x2, sc1, sh1, n1w, w_packed, cos, s1, s2, qnw, knw)


_BISECT_STEPS = 4
_BISECT_MAX_ROUNDS = 96


def _dsa_kernel(qa_ref, ka_ref, va_ref, qi_ref, ki_ref, wi_ref, nw_ref, o_ref, jstar_ref,
                *, n_sel, tq, s_len):
    q0 = pl.program_id(1) * tq
    ki = ki_ref[...]
    wi = wi_ref[...]
    score = None
    for hh in range(N_IDX_HEADS):
        d = lax.dot_general(qi_ref[:, hh * _HP:(hh + 1) * _HP], ki, NT_DIMS,
                            preferred_element_type=F32)
        term = wi[:, hh:hh + 1] * jnp.maximum(d, 0.0)
        score = term if score is None else score + term
    kpos = lax.broadcasted_iota(I32, (tq, s_len), 1)
    qpos = q0 + lax.broadcasted_iota(I32, (tq, s_len), 0)
    causal = kpos <= qpos
    score = jnp.where(causal, score, -jnp.inf)

    n_causal = (q0 + 1 + lax.broadcasted_iota(I32, (tq, 1), 0)).astype(F32)
    kk = jnp.minimum(n_causal, float(n_sel))
    lo0 = jnp.min(jnp.where(causal, score, jnp.inf), axis=-1, keepdims=True)
    hi0 = jnp.max(score, axis=-1, keepdims=True)
    hi0 = jnp.where(n_causal <= float(n_sel), lo0, hi0)

    def search_round(carry):
        lo, hi, _, it = carry
        for _ in range(_BISECT_STEPS):
            mid = lo + 0.5 * (hi - lo)
            stuck = jnp.logical_or(mid <= lo, mid >= hi)
            piv = jnp.where(stuck, lo, mid)
            cnt = jnp.sum(jnp.where(score > piv, 1.0, 0.0), axis=-1, keepdims=True)
            below = cnt < kk
            hi_n = jnp.where(below, piv, hi)
            lo = jnp.where(below, lo, jnp.where(stuck, hi, piv))
            hi = hi_n
        top = jnp.max(jnp.where(score <= hi, score, -jnp.inf), axis=-1, keepdims=True)
        bot = jnp.min(jnp.where(score >= lo, score, jnp.inf), axis=-1, keepdims=True)
        open_rows = jnp.max(jnp.where(top > bot, 1, 0))
        return bot, top, open_rows, it + 1

    def search_open(carry):
        return jnp.logical_and(carry[2] > 0, carry[3] < _BISECT_MAX_ROUNDS)

    open0 = jnp.max(jnp.where(hi0 > lo0, 1, 0))
    _, theta, _, _ = lax.while_loop(search_open, search_round, (lo0, hi0, open0, jnp.int32(0)))

    above = score > theta
    tied = score == theta
    need = kk - jnp.sum(jnp.where(above, 1.0, 0.0), axis=-1, keepdims=True)
    n_tied = jnp.sum(jnp.where(tied, 1.0, 0.0), axis=-1, keepdims=True)
    jstar_ref[...] = jnp.full(jstar_ref.shape, s_len, I32)

    @pl.when(jnp.max(jnp.where(n_tied > need, 1, 0)) > 0)
    def _():
        def step(_, c):
            lo_i, hi_i = c
            mid = (lo_i + hi_i) >> 1
            cnt = jnp.sum(jnp.where(tied, jnp.where(kpos <= mid, 1.0, 0.0), 0.0),
                          axis=-1, keepdims=True)
            ok = cnt >= need
            return jnp.where(ok, lo_i, mid), jnp.where(ok, mid, hi_i)

        n_steps = int(np.ceil(np.log2(s_len))) + 1
        _, hi_i = lax.fori_loop(0, n_steps, step,
                                (jnp.full((tq, 1), -1, I32), jnp.full((tq, 1), s_len - 1, I32)))
        jstar_ref[...] = jnp.broadcast_to(hi_i, jstar_ref.shape)

    jstar = jstar_ref[:, 0:1]
    bias = jnp.where(tied, jnp.where(kpos <= jstar, 0.0, -jnp.inf), jnp.where(above, 0.0, -jnp.inf))

    rep = N_ATTN_HEADS // N_KV_HEADS
    for g in range(N_KV_HEADS):
        k_g = ka_ref[:, g * _HP:(g + 1) * _HP]
        v_g = va_ref[:, g * _HP:(g + 1) * _HP]
        for r in range(rep):
            hh = g * rep + r
            sl = slice(hh * _HP, (hh + 1) * _HP)
            lg = lax.dot_general(qa_ref[:, sl], k_g, NT_DIMS, preferred_element_type=F32) + bias
            m = jnp.max(lg, axis=-1, keepdims=True)
            p = jnp.exp(lg - m)
            den = jnp.sum(p, axis=-1, keepdims=True)
            o = jnp.dot(p.astype(BF16), v_g, preferred_element_type=F32) / den
            ssq = jnp.sum(o * o, axis=-1, keepdims=True)
            on = o * lax.rsqrt(ssq * (1.0 / ATTN_HEAD_DIM) + NORM_EPS) * nw_ref[hh:hh + 1, :]
            o_ref[:, sl] = on.astype(BF16)


def _dsa(qa, ka, va, qi, ki, wi, nw, bsz, seq):
    t = qa.shape[0]
    tq = min(128, seq)
    nq = seq // tq
    n_sel = min(TOPK_KEYS, seq // 4)
    qrow = lambda w: pl.BlockSpec((tq, w), lambda b, j: (b * nq + j, 0))
    krow = lambda w: pl.BlockSpec((seq, w), lambda b, j: (b, 0))
    return pl.pallas_call(
        functools.partial(_dsa_kernel, n_sel=n_sel, tq=tq, s_len=seq),
        grid=(bsz, nq),
        in_specs=[qrow(W_QA), krow(W_KA), krow(W_KA), qrow(W_QI), krow(_HP), qrow(_HP),
                  pl.BlockSpec((N_ATTN_HEADS, _HP), lambda b, j: (0, 0))],
        out_specs=qrow(W_QA),
        out_shape=jax.ShapeDtypeStruct((t, W_QA), BF16),
        scratch_shapes=[pltpu.VMEM((tq, LANE), I32)],
        compiler_params=_cparams("arbitrary", "arbitrary"),
        name="dsa",
    )(qa, ka, va, qi, ki, wi, nw)


_GLA_LEVELS = 6
_GLA_ROWS = 256


def _gla_consts():
    c = GLA_CHUNK
    hk = N_GLA_HEADS * GLA_KEY_DIM
    i = np.arange(c)
    cum = [np.tril(np.ones((c, c)))]
    rq, pm = [], []
    t = i[:, None]
    s = (np.arange(hk) % GLA_KEY_DIM)[None, :]
    pm.append((t == s).astype(np.float32))
    for lvl in range(1, _GLA_LEVELS + 1):
        blk, half = 1 << lvl, 1 << (lvl - 1)
        split = (i // blk) * blk + half - 1
        cum.append((np.arange(c)[None, :] <= split[:, None]).astype(np.float64))
        right = ((i // half) % 2 == 1)
        rq.append(np.broadcast_to(right[:, None], (c, hk)).astype(np.float32))
        pm.append(((t // blk == s // blk) & ((t // half) % 2 == 1) & ((s // half) % 2 == 0))
                  .astype(np.float32))
    r = np.arange(hk)
    bd_k = (r[:, None] // GLA_KEY_DIM == r[None, :] // GLA_KEY_DIM).astype(np.float32)
    cv = np.arange(N_GLA_HEADS * GLA_VAL_DIM)
    bd_v = (r[:, None] // GLA_KEY_DIM == cv[None, :] // GLA_VAL_DIM).astype(np.float32)
    return (jnp.asarray(np.concatenate(cum, 0), BF16), jnp.asarray(np.stack(rq)),
            jnp.asarray(np.stack(pm)), jnp.asarray(bd_k), jnp.asarray(bd_v),
            jnp.asarray(bd_v.T.copy()))


def _gla_kernel(qg_ref, kg_ref, vg_ref, gk_ref, go_ref, wgk_ref, bgk_ref, gnw_ref,
                cum_ref, rq_ref, pm_ref, bdk_ref, bdv_ref, bdvt_ref, o_ref, st_ref):
    c = GLA_CHUNK

    @pl.when(pl.program_id(1) == 0)
    def _():
        st_ref[...] = jnp.zeros(st_ref.shape, F32)

    bdk = bdk_ref[...]
    cum = cum_ref[...]
    for ci in range(_GLA_ROWS // c):
        sl = slice(ci * c, (ci + 1) * c)
        q = qg_ref[sl, :] * (GLA_KEY_DIM ** -0.5)
        k = kg_ref[sl, :]
        v = vg_ref[sl, :]
        z = jnp.dot(gk_ref[sl, :].astype(BF16), wgk_ref[...], preferred_element_type=F32) + bgk_ref[...]
        la = (jnp.minimum(z, 0.0) - jnp.log1p(jnp.exp(-jnp.abs(z)))) * (1.0 / GLA_GATE_TAU)
        la_hi = la.astype(BF16)
        la_lo = (la - la_hi.astype(F32)).astype(BF16)
        cb = (jnp.dot(cum, la_hi, preferred_element_type=F32)
              + jnp.dot(cum, la_lo, preferred_element_type=F32))
        b = cb[0:c]
        kbd = (jnp.concatenate([k] * N_GLA_HEADS, axis=0) * bdk).astype(BF16)
        a = lax.dot_general(q.astype(BF16), kbd, NT_DIMS, preferred_element_type=F32) * pm_ref[0]
        for lvl in range(1, _GLA_LEVELS + 1):
            beta = cb[lvl * c:(lvl + 1) * c]
            rq = rq_ref[lvl - 1]
            qs = q * jnp.exp(jnp.minimum(b - beta, 0.0)) * rq
            ks = k * jnp.exp(jnp.minimum(beta - b, 0.0)) * (1.0 - rq)
            ksbd = (jnp.concatenate([ks] * N_GLA_HEADS, axis=0) * bdk).astype(BF16)
            a = a + lax.dot_general(qs.astype(BF16), ksbd, NT_DIMS,
                                    preferred_element_type=F32) * pm_ref[lvl]
        vbd = (jnp.concatenate([v] * N_GLA_HEADS, axis=0) * bdv_ref[...]).astype(BF16)
        intra = jnp.dot(a.astype(BF16), vbd, preferred_element_type=F32)
        st = st_ref[...]
        inter = lax.dot_general((q * jnp.exp(b)).astype(BF16), st.astype(BF16), NT_DIMS,
                                preferred_element_type=F32)
        o = intra + inter
        b_end = b[c - 1:c, :]
        k_dec = (k * jnp.exp(b_end - b)).astype(BF16)
        upd = lax.dot_general(v.astype(BF16), k_dec, TN_DIMS, preferred_element_type=F32)
        st_ref[...] = st * jnp.exp(b_end) + upd * bdvt_ref[...]
        for hh in range(N_GLA_HEADS):
            hs = slice(hh * GLA_VAL_DIM, (hh + 1) * GLA_VAL_DIM)
            oh = o[:, hs]
            ms = jnp.mean(oh * oh, axis=-1, keepdims=True)
            on = oh * lax.rsqrt(ms + NORM_EPS) * gnw_ref[:, hs]
            o_ref[sl, hs] = (on * _silu(go_ref[sl, hs])).astype(BF16)


def _gla(qg, kg, vg, gk, go, wgk, bgk, gnw, bsz, seq):
    t = qg.shape[0]
    rows = _GLA_ROWS
    assert seq % rows == 0
    nj = seq // rows
    consts = _gla_consts()
    row = lambda w: pl.BlockSpec((rows, w), lambda b, j: (b * nj + j, 0))
    full = lambda a: pl.BlockSpec(a.shape, lambda b, j: (0,) * a.ndim)
    args = (qg, kg, vg, gk, go, wgk, bgk, gnw) + consts
    return pl.pallas_call(
        _gla_kernel,
        grid=(bsz, nj),
        in_specs=[row(W_GQ), row(W_GQ), row(W_GV), row(_HP), row(W_GV)] + [full(a) for a in args[5:]],
        out_specs=row(W_GV),
        out_shape=jax.ShapeDtypeStruct((t, W_GV), BF16),
        scratch_shapes=[pltpu.VMEM((W_GV, W_GQ), F32)],
        compiler_params=_cparams("arbitrary", "arbitrary"),
        name="gla",
    )(*args)


def _outproj_kernel(attn_ref, gla_ref, x_ref, g1_ref, sc_ref, sh_ref, n2_ref, wa_ref, wb_ref,
                    wr_ref, br_ref, tri_ref,
                    x1_ref, h2_ref, idx_ref, wsel_ref, rank_ref, ccol_ref, crow_ref, *, tm):
    @pl.when(pl.program_id(0) == 0)
    def _():
        ccol_ref[...] = jnp.zeros(ccol_ref.shape, F32)
        crow_ref[...] = jnp.zeros(crow_ref.shape, F32)

    mixed = (jnp.dot(attn_ref[...], wa_ref[...], preferred_element_type=F32)
             + jnp.dot(gla_ref[...], wb_ref[...], preferred_element_type=F32))
    x1 = x_ref[...] + g1_ref[0] * mixed
    x1_ref[...] = x1
    ms = jnp.mean(x1 * x1, axis=-1, keepdims=True)
    h2 = x1 * lax.rsqrt(ms + NORM_EPS) * n2_ref[...]
    h2 = h2 * (1.0 + sc_ref[0]) + sh_ref[0]
    h2_ref[...] = h2

    logits = lax.dot_general(wr_ref[...], h2.astype(BF16), NT_DIMS, preferred_element_type=F32)
    scores = jax.nn.sigmoid(logits)
    biased = scores + jnp.concatenate([br_ref[...]] * (tm // LANE), axis=1)
    per_group = N_EXPERTS // N_EXPERT_GROUPS
    gs = []
    for g in range(N_EXPERT_GROUPS):
        blk = biased[g * per_group:(g + 1) * per_group]
        m1 = jnp.max(blk, axis=0, keepdims=True)
        n1 = jnp.sum(jnp.where(blk == m1, 1.0, 0.0), axis=0, keepdims=True)
        m2 = jnp.max(jnp.where(blk < m1, blk, -jnp.inf), axis=0, keepdims=True)
        gs.append(m1 + jnp.where(n1 >= 2.0, m1, m2))
    gscore = jnp.concatenate(gs, axis=0)
    gi = lax.broadcasted_iota(I32, gscore.shape, 0)
    gsel = jnp.zeros(gscore.shape, F32)
    for _ in range(TOPK_GROUPS):
        m = jnp.max(gscore, axis=0, keepdims=True)
        first = jnp.min(jnp.where(gscore == m, gi, N_EXPERT_GROUPS), axis=0, keepdims=True)
        hit = gi == first
        gsel = jnp.where(hit, 1.0, gsel)
        gscore = jnp.where(hit, -jnp.inf, gscore)
    gmask = jnp.concatenate(
        [jnp.broadcast_to(gsel[g:g + 1], (per_group, tm)) for g in range(N_EXPERT_GROUPS)], axis=0)
    masked = jnp.where(gmask > 0.0, biased, -jnp.inf)
    ei = lax.broadcasted_iota(I32, masked.shape, 0)
    chosen = jnp.zeros(masked.shape, F32)
    idxs, ws = [], []
    for _ in range(TOPK_EXPERTS):
        m = jnp.max(masked, axis=0, keepdims=True)
        first = jnp.min(jnp.where(masked == m, ei, N_EXPERTS), axis=0, keepdims=True)
        hit = ei == first
        idxs.append(first)
        ws.append(jnp.sum(jnp.where(hit, scores, 0.0), axis=0, keepdims=True))
        chosen = jnp.where(hit, 1.0, chosen)
        masked = jnp.where(hit, -jnp.inf, masked)
    wsum = ws[0]
    for w in ws[1:]:
        wsum = wsum + w
    idx_ref[...] = jnp.concatenate(idxs, axis=0)
    wsel_ref[...] = jnp.concatenate([w / wsum * ROUTED_SCALE for w in ws], axis=0)

    chosen_b = chosen.astype(BF16)
    before = jnp.dot(chosen_b, tri_ref[...], preferred_element_type=F32)
    seen = ccol_ref[...]
    pos = before + jnp.concatenate([seen] * (tm // LANE), axis=1)
    rank_ref[...] = jnp.concatenate(
        [jnp.sum(jnp.where(ei == i, pos, 0.0), axis=0, keepdims=True) for i in idxs],
        axis=0).astype(I32)
    ccol_ref[...] = seen + jnp.sum(chosen, axis=1, keepdims=True)
    crow_ref[...] = crow_ref[...] + lax.dot_general(
        jnp.ones((SUBLANE, tm), BF16), chosen_b, NT_DIMS, preferred_element_type=F32)


def _outproj(attn, gla, x2, g1, sc2, sh2, n2w, wa, wb, wr_t, br, seq):
    t, d = x2.shape
    tm = min(512, seq)
    per_b = seq // tm
    row = lambda w: pl.BlockSpec((tm, w), lambda i: (i, 0))
    col = lambda: pl.BlockSpec((TOPK_EXPERTS, tm), lambda i: (0, i))
    mod = pl.BlockSpec((1, 1, d), lambda i: (i // per_b, 0, 0))
    full = lambda a: pl.BlockSpec(a.shape, lambda i: (0,) * a.ndim)
    tri = jnp.asarray(np.triu(np.ones((tm, tm), np.float32), 1), BF16)
    br_col = jnp.broadcast_to(br.reshape(N_EXPERTS, 1).astype(F32), (N_EXPERTS, LANE))
    n2 = n2w.reshape(1, d)
    sel = jax.ShapeDtypeStruct((TOPK_EXPERTS, t), I32)
    return pl.pallas_call(
        functools.partial(_outproj_kernel, tm=tm),
        grid=(t // tm,),
        in_specs=[row(W_QA), row(W_GV), row(d), mod, mod, mod, full(n2), full(wa), full(wb),
                  full(wr_t), full(br_col), full(tri)],
        out_specs=[row(d), row(d), col(), col(), col(),
                   pl.BlockSpec((N_EXPERTS, LANE), lambda i: (0, 0)),
                   pl.BlockSpec((SUBLANE, N_EXPERTS), lambda i: (0, 0))],
        out_shape=[jax.ShapeDtypeStruct((t, d), F32), jax.ShapeDtypeStruct((t, d), F32),
                   sel, jax.ShapeDtypeStruct((TOPK_EXPERTS, t), F32), sel,
                   jax.ShapeDtypeStruct((N_EXPERTS, LANE), F32),
                   jax.ShapeDtypeStruct((SUBLANE, N_EXPERTS), F32)],
        compiler_params=_cparams("arbitrary"),
        name="outproj",
    )(attn, gla, x2, g1, sc2, sh2, n2, wa, wb, wr_t, br_col, tri)


def _plan_kernel(ccol_ref, crow_ref, idx_ref, rank_ref, dest_ref, blk_ref, nblk_ref, *, n_blocks):
    m = float(EXPERT_ROWS)
    pad_row = jnp.floor((crow_ref[0:1, :] + (m - 1.0)) * (1.0 / m)) * m
    pad_col = jnp.floor((ccol_ref[:, 0:1] + (m - 1.0)) * (1.0 / m)) * m
    er = lax.broadcasted_iota(I32, (N_EXPERTS, N_EXPERTS), 0)
    ec = lax.broadcasted_iota(I32, (N_EXPERTS, N_EXPERTS), 1)
    start_col = jnp.sum(jnp.where(ec < er, pad_row, 0.0), axis=1, keepdims=True)
    end_col = start_col + pad_col
    first_row = (lax.broadcasted_iota(I32, (N_EXPERTS, n_blocks), 1) * EXPERT_ROWS).astype(F32)
    owner = jnp.sum(jnp.where(end_col <= first_row, 1.0, 0.0), axis=0, keepdims=True)
    blk_ref[...] = jnp.minimum(owner, float(N_EXPERTS - 1)).astype(I32)
    total = jnp.sum(pad_row, axis=1, keepdims=True)
    nblk_ref[...] = jnp.broadcast_to(total * (1.0 / m), nblk_ref.shape).astype(I32)
    tm = idx_ref.shape[1]
    ei = lax.broadcasted_iota(I32, (N_EXPERTS, tm), 0)
    rows = []
    for kc in range(TOPK_EXPERTS):
        hit = ei == idx_ref[kc:kc + 1, :]
        rows.append(jnp.sum(jnp.where(hit, start_col, 0.0), axis=0, keepdims=True))
    dest_ref[...] = jnp.concatenate(rows, axis=0).astype(I32) + rank_ref[...]


def _plan(ccol, crow, idx, rank, n_blocks):
    t = idx.shape[1]
    tm = min(1024, t)
    col = pl.BlockSpec((TOPK_EXPERTS, tm), lambda i: (0, i))
    full = lambda a: pl.BlockSpec(a.shape, lambda i: (0,) * a.ndim)
    return pl.pallas_call(
        functools.partial(_plan_kernel, n_blocks=n_blocks),
        grid=(t // tm,),
        in_specs=[full(ccol), full(crow), col, col],
        out_specs=[col, pl.BlockSpec((1, n_blocks), lambda i: (0, 0)),
                   pl.BlockSpec((1, LANE), lambda i: (0, 0))],
        out_shape=[jax.ShapeDtypeStruct((TOPK_EXPERTS, t), I32),
                   jax.ShapeDtypeStruct((1, n_blocks), I32),
                   jax.ShapeDtypeStruct((1, LANE), I32)],
        compiler_params=_cparams("arbitrary"),
        name="plan",
    )(ccol, crow, idx, rank)


def _row_copy(src, src_row, dst, dst_row, sem):
    return pltpu.make_async_copy(src.at[pl.ds(src_row * ROW_SLABS, ROW_SLABS), :],
                                 dst.at[pl.ds(dst_row * ROW_SLABS, ROW_SLABS), :], sem)


def _scatter_kernel(dest_ref, h_ref, xb_in_ref, xb_ref, sem, *, tm):
    del xb_in_ref

    def issue(tok, carry):
        for kc in range(TOPK_EXPERTS):
            _row_copy(h_ref, tok, xb_ref, dest_ref[kc, tok], sem).start()
        return carry

    lax.fori_loop(0, tm, issue, 0)

    def drain(tok, carry):
        for kc in range(TOPK_EXPERTS):
            _row_copy(h_ref, tok, xb_ref, dest_ref[kc, tok], sem).wait()
        return carry

    lax.fori_loop(0, tm, drain, 0)


def _scatter(dest, h_slabs, xb_zero):
    t = dest.shape[1]
    tm = min(128, t)
    return pl.pallas_call(
        functools.partial(_scatter_kernel, tm=tm),
        grid=(t // tm,),
        in_specs=[pl.BlockSpec((TOPK_EXPERTS, tm), lambda i: (0, i), memory_space=pltpu.SMEM),
                  pl.BlockSpec((tm * ROW_SLABS, LANE), lambda i: (i, 0)),
                  pl.BlockSpec(memory_space=pl.ANY)],
        out_specs=pl.BlockSpec(memory_space=pl.ANY),
        out_shape=jax.ShapeDtypeStruct(xb_zero.shape, F32),
        scratch_shapes=[pltpu.SemaphoreType.DMA],
        input_output_aliases={2: 0},
        compiler_params=_cparams("arbitrary"),
        name="scatter",
    )(dest, h_slabs, xb_zero)


def _experts_kernel(blk_ref, nblk_ref, x_ref, wg_ref, wu_ref, wd_ref, y_ref, wgb, wub, wdb):
    i = pl.program_id(0)
    m = EXPERT_ROWS

    @pl.when(i < nblk_ref[0])
    def _():
        prev = blk_ref[jnp.maximum(i - 1, 0)]

        @pl.when(jnp.logical_or(i == 0, blk_ref[i] != prev))
        def _():
            wgb[...] = wg_ref[0].astype(BF16)
            wub[...] = wu_ref[0].astype(BF16)
            wdb[...] = wd_ref[0].astype(BF16)

        x = jnp.concatenate(
            [x_ref[pl.ds(s, m, stride=ROW_SLABS), :].astype(BF16) for s in range(ROW_SLABS)], axis=1)
        g = jnp.dot(x, wgb[...], preferred_element_type=F32)
        u = jnp.dot(x, wub[...], preferred_element_type=F32)
        y = jnp.dot((_silu(g) * u).astype(BF16), wdb[...], preferred_element_type=F32)
        for s in range(ROW_SLABS):
            y_ref[pl.ds(s, m, stride=ROW_SLABS), :] = y[:, s * LANE:(s + 1) * LANE]


def _experts(blk, nblk, xb, we_gate, we_up, we_down, n_blocks):
    m = EXPERT_ROWS
    e, d, f = we_gate.shape
    live = lambda i, blk, nblk: jnp.minimum(i, jnp.maximum(nblk[0] - 1, 0))
    grid_spec = pltpu.PrefetchScalarGridSpec(
        num_scalar_prefetch=2,
        grid=(n_blocks,),
        in_specs=[pl.BlockSpec((m * ROW_SLABS, LANE), lambda i, blk, nblk: (live(i, blk, nblk), 0)),
                  pl.BlockSpec((1, d, f), lambda i, blk, nblk: (blk[live(i, blk, nblk)], 0, 0)),
                  pl.BlockSpec((1, d, f), lambda i, blk, nblk: (blk[live(i, blk, nblk)], 0, 0)),
                  pl.BlockSpec((1, f, d), lambda i, blk, nblk: (blk[live(i, blk, nblk)], 0, 0))],
        out_specs=pl.BlockSpec((m * ROW_SLABS, LANE), lambda i, blk, nblk: (live(i, blk, nblk), 0)),
        scratch_shapes=[pltpu.VMEM((d, f), BF16), pltpu.VMEM((d, f), BF16), pltpu.VMEM((f, d), BF16)],
    )
    return pl.pallas_call(
        _experts_kernel,
        grid_spec=grid_spec,
        out_shape=jax.ShapeDtypeStruct(xb.shape, F32),
        input_output_aliases={2: 0},
        compiler_params=_cparams("arbitrary"),
        name="experts",
    )(blk, nblk, xb, we_gate, we_up, we_down)


def _combine_kernel(dest_ref, yb_ref, w_ref, h_ref, x1_ref, g2_ref, wsg_ref, wsu_ref, wsd_ref,
                    o_ref, buf, sem, *, tm):
    def issue(tok, carry):
        for kc in range(TOPK_EXPERTS):
            _row_copy(yb_ref, dest_ref[kc, tok], buf, kc * tm + tok, sem).start()
        return carry

    lax.fori_loop(0, tm, issue, 0)

    hb = h_ref[...].astype(BF16)
    g = jnp.dot(hb, wsg_ref[...], preferred_element_type=F32)
    u = jnp.dot(hb, wsu_ref[...], preferred_element_type=F32)
    shared = jnp.dot((_silu(g) * u).astype(BF16), wsd_ref[...], preferred_element_type=F32)

    def drain(tok, carry):
        for kc in range(TOPK_EXPERTS):
            _row_copy(yb_ref, dest_ref[kc, tok], buf, kc * tm + tok, sem).wait()
        return carry

    lax.fori_loop(0, tm, drain, 0)

    w = w_ref[...]
    parts = []
    for s in range(ROW_SLABS):
        acc = None
        for kc in range(TOPK_EXPERTS):
            rows = buf[pl.ds(kc * tm * ROW_SLABS + s, tm, stride=ROW_SLABS), :]
            term = w[:, kc:kc + 1] * rows
            acc = term if acc is None else acc + term
        parts.append(acc)
    routed = jnp.concatenate(parts, axis=1)
    o_ref[...] = x1_ref[...] + g2_ref[0] * (routed + shared)


def _combine(dest, yb, w_tok, h2, x1, g2, wsg, wsu, wsd, seq):
    t, d = h2.shape
    tm = min(128, seq)
    per_b = seq // tm
    row = lambda w: pl.BlockSpec((tm, w), lambda i: (i, 0))
    full = lambda a: pl.BlockSpec(a.shape, lambda i: (0,) * a.ndim)
    return pl.pallas_call(
        functools.partial(_combine_kernel, tm=tm),
        grid=(t // tm,),
        in_specs=[pl.BlockSpec((TOPK_EXPERTS, tm), lambda i: (0, i), memory_space=pltpu.SMEM),
                  pl.BlockSpec(memory_space=pl.ANY),
                  row(TOPK_EXPERTS), row(d), row(d),
                  pl.BlockSpec((1, 1, d), lambda i: (i // per_b, 0, 0)),
                  full(wsg), full(wsu), full(wsd)],
        out_specs=row(d),
        out_shape=jax.ShapeDtypeStruct((t, d), F32),
        scratch_shapes=[pltpu.VMEM((TOPK_EXPERTS * tm * ROW_SLABS, LANE), F32),
                        pltpu.SemaphoreType.DMA],
        compiler_params=_cparams("arbitrary"),
        name="combine",
    )(dest, yb, w_tok, h2, x1, g2, wsg, wsu, wsd)


def _layer(x2, cond_mod, positions_tables, p, bsz, seq):
    t, d = x2.shape
    cos, s1, s2 = positions_tables
    sh1, sc1, g1, sh2, sc2, g2 = [m.reshape(bsz, 1, d) for m in jnp.split(cond_mod, 6, axis=-1)]

    lane_pad = lambda w: jnp.pad(w.astype(F32), ((0, 0), (0, _HP - w.shape[-1])))
    qnw = lane_pad(p["q_norm_w"].reshape(1, ATTN_HEAD_DIM))
    knw = lane_pad(p["k_norm_w"].reshape(1, ATTN_HEAD_DIM))
    qa, ka, va, qi, ki, wi, qg, kg, vg, gk, go = _inproj(
        x2, sc1, sh1, p["norm1_w"].reshape(1, d), _pack_w_in(p["w_in"]), cos, s1, s2, qnw, knw, seq)

    attn = _dsa(qa, ka, va, qi, ki, wi, lane_pad(p["attn_out_norm_w"]), bsz, seq)

    wgk = jnp.pad(p["w_gk2"], ((0, _HP - GLA_GATE_RANK), (0, 0))).astype(BF16)
    gla = _gla(qg, kg, vg, gk, go, wgk, p["b_gk"].reshape(1, W_GQ).astype(F32),
               p["gla_norm_w"].reshape(1, W_GV).astype(F32), bsz, seq)

    w_out = p["w_out"]
    n_attn = N_ATTN_HEADS * ATTN_HEAD_DIM
    wa = jnp.pad(w_out[:n_attn].reshape(N_ATTN_HEADS, ATTN_HEAD_DIM, d),
                 ((0, 0), (0, _HP - ATTN_HEAD_DIM), (0, 0))).reshape(W_QA, d).astype(BF16)
    wb = w_out[n_attn:].astype(BF16)
    x1, h2, idx, wsel, rank, ccol, crow = _outproj(
        attn, gla, x2, g1, sc2, sh2, p["norm2_w"], wa, wb,
        p["w_router"].T.astype(BF16), p["b_router"], seq)

    n_blocks = (t * TOPK_EXPERTS) // EXPERT_ROWS + N_EXPERTS
    dest, blk, nblk = _plan(ccol, crow, idx, rank, n_blocks)

    n_rows = n_blocks * EXPERT_ROWS
    xb = _scatter(dest, h2.reshape(t * ROW_SLABS, LANE),
                  jnp.zeros((n_rows * ROW_SLABS, LANE), F32))
    yb = _experts(blk.reshape(n_blocks), nblk.reshape(LANE)[:1], xb,
                  p["we_gate"], p["we_up"], p["we_down"], n_blocks)
    return _combine(dest, yb, wsel.T, h2, x1, g2, p["ws_gate"].astype(BF16),
                    p["ws_up"].astype(BF16), p["ws_down"].astype(BF16), seq)


def kernel(x, c, positions, norm1_w, norm2_w, w_ada, b_ada, w_in, q_norm_w, k_norm_w,
           attn_out_norm_w, w_gk2, b_gk, gla_norm_w, w_out, w_router, b_router,
           we_gate, we_up, we_down, ws_gate, ws_up, ws_down):
    bsz, seq, d = x.shape
    depth = w_ada.shape[0]
    x2 = x.reshape(bsz * seq, d)
    tables = _rope_tables(positions)
    stacked = dict(norm1_w=norm1_w, norm2_w=norm2_w, w_in=w_in, q_norm_w=q_norm_w, k_norm_w=k_norm_w,
                   attn_out_norm_w=attn_out_norm_w, w_gk2=w_gk2, b_gk=b_gk, gla_norm_w=gla_norm_w,
                   w_out=w_out, w_router=w_router, b_router=b_router, we_gate=we_gate, we_up=we_up,
                   we_down=we_down, ws_gate=ws_gate, ws_up=ws_up, ws_down=ws_down)
    for l in range(depth):
        mod = _ada(c.astype(F32), w_ada[l], b_ada[l])
        x2 = _layer(x2, mod, tables, {k: v[l] for k, v in stacked.items()}, bsz, seq)
    return x2.reshape(bsz, seq, d)
```

```python
import functools

import numpy as np
import jax
import jax.numpy as jnp
from jax import lax
from jax.experimental import pallas as pl
from jax.experimental.pallas import tpu as pltpu

F32 = jnp.float32
BF16 = jnp.bfloat16
I32 = jnp.int32

LANE = 128
SUBLANE = 8
VMEM_LIMIT = 48 * 1024 * 1024

N_ATTN_HEADS = 8
ATTN_HEAD_DIM = 64
N_KV_HEADS = 2
N_IDX_HEADS = 4
IDX_HEAD_DIM = 64
TOPK_KEYS = 256
N_GLA_HEADS = 4
GLA_KEY_DIM = 64
GLA_VAL_DIM = 128
GLA_GATE_RANK = 16
GLA_GATE_TAU = 16.0
GLA_CHUNK = 64
ROPE_THETA = 500000.0
ROPE_FRACTION = 4
N_EXPERTS = 256
N_EXPERT_GROUPS = 8
TOPK_GROUPS = 4
TOPK_EXPERTS = 8
EXPERT_DIM = 256
ROUTED_SCALE = 2.5
NORM_EPS = 1e-6
IN_SIZES = (512, 128, 128, 256, 64, 4, 256, 256, 512, 16, 512)

U32 = jnp.uint32
EXPERT_ROWS = 256
NT_DIMS = (((1,), (1,)), ((), ()))
TN_DIMS = (((0,), (0,)), ((), ()))


def _cparams(*sem):
    return pltpu.CompilerParams(dimension_semantics=sem, vmem_limit_bytes=VMEM_LIMIT)


def _silu(v):
    return v * jax.nn.sigmoid(v)


def _pack_row_halves(v):
    half = v.shape[1] // 2
    bits = lambda a: lax.bitcast_convert_type(a.astype(jnp.bfloat16).astype(F32), U32)
    return bits(v[:, half:]) | (bits(v[:, :half]) >> 16)


def _unpack_row_halves(w):
    lo = lax.bitcast_convert_type(w << 16, F32)
    hi = lax.bitcast_convert_type(w & jnp.uint32(0xFFFF0000), F32)
    return jnp.concatenate([lo, hi], axis=1)


def _store_slabs(ref, row0, n_rows, w):
    n_slabs = w.shape[1] // LANE
    for sb in range(n_slabs):
        ref[pl.ds(row0 * n_slabs + sb, n_rows, stride=n_slabs), :] = w[:, sb * LANE:(sb + 1) * LANE]


def _load_slabs(ref, row0, n_rows, n_slabs):
    return jnp.concatenate(
        [ref[pl.ds(row0 * n_slabs + sb, n_rows, stride=n_slabs), :] for sb in range(n_slabs)], axis=1)


def _ada_kernel(c_ref, w_ref, b_ref, o_ref):
    cond = _silu(c_ref[...])
    o_ref[...] = jnp.dot(cond.astype(BF16), w_ref[...].astype(BF16),
                         preferred_element_type=F32) + b_ref[...]


def _ada(c, w, b):
    bsz, d = c.shape
    n = w.shape[1]
    tn = 1536
    return pl.pallas_call(
        _ada_kernel,
        grid=(n // tn,),
        in_specs=[pl.BlockSpec((bsz, d), lambda i: (0, 0)),
                  pl.BlockSpec((d, tn), lambda i: (0, i)),
                  pl.BlockSpec((1, tn), lambda i: (0, i))],
        out_specs=pl.BlockSpec((bsz, tn), lambda i: (0, i)),
        out_shape=jax.ShapeDtypeStruct((bsz, n), F32),
        compiler_params=_cparams("arbitrary"),
        name="ada",
    )(c, w, b.reshape(1, n))


def _rope_kernel(pos_ref, invf_ref, m1_ref, m2_ref, cos_ref, s1_ref, s2_ref):
    ang = pos_ref[...] * invf_ref[...]
    s = jnp.sin(ang)
    cos_ref[...] = jnp.cos(ang)
    s1_ref[...] = -s * m1_ref[...]
    s2_ref[...] = s * m2_ref[...]


def _rope_tables(positions):
    t = positions.size
    rot = ATTN_HEAD_DIM // ROPE_FRACTION
    half = rot // 2
    inv_freq = jnp.power(jnp.float32(ROPE_THETA), -jnp.arange(half, dtype=F32) / half)
    lane = np.arange(LANE) % ATTN_HEAD_DIM
    invf = jnp.where(lane < rot, inv_freq[lane % half], 0.0).astype(F32).reshape(1, LANE)
    m1 = jnp.asarray((lane < half).astype(np.float32)).reshape(1, LANE)
    m2 = jnp.asarray(((lane >= half) & (lane < rot)).astype(np.float32)).reshape(1, LANE)
    pos = jnp.broadcast_to(positions.reshape(t, 1).astype(F32), (t, LANE))
    tm = min(2048, t)
    row = pl.BlockSpec((tm, LANE), lambda i: (i, 0))
    one = pl.BlockSpec((1, LANE), lambda i: (0, 0))
    sds = jax.ShapeDtypeStruct((t, LANE), F32)
    return pl.pallas_call(
        _rope_kernel,
        grid=(t // tm,),
        in_specs=[row, one, one, one],
        out_specs=[row, row, row],
        out_shape=[sds, sds, sds],
        compiler_params=_cparams("arbitrary"),
        name="rope",
    )(pos, invf, m1, m2)


_HP = LANE
W_QA = N_ATTN_HEADS * _HP
W_KA = N_KV_HEADS * _HP
W_QI = N_IDX_HEADS * _HP
W_GQ = N_GLA_HEADS * GLA_KEY_DIM
W_GV = N_GLA_HEADS * GLA_VAL_DIM
_GROUPS = (("qa", W_QA), ("ka", W_KA), ("qi", W_QI), ("ki", _HP),
           ("gqk", 2 * W_GQ), ("gv", W_GV), ("go", W_GV), ("gk", _HP))
_OFF = {}
_o = 0
for _n, _w in _GROUPS:
    _OFF[_n] = (_o, _o + _w)
    _o += _w
W_IN_PACKED = _o
W_IN_T = W_KA + SUBLANE
_DSA_KEY_CHUNK = 512


def _pad_heads(w, n_heads, hd):
    d = w.shape[0]
    w = w.reshape(d, n_heads, hd)
    return jnp.pad(w, ((0, 0), (0, 0), (0, _HP - hd))).reshape(d, n_heads * _HP)


def _pack_w_in(w_in):
    ends = np.cumsum(IN_SIZES)
    starts = ends - np.asarray(IN_SIZES)
    qa, ka, va, qi, ki, wi, qg, kg, vg, gk, go = [w_in[:, a:b] for a, b in zip(starts, ends)]
    d = w_in.shape[0]
    pad_to = lambda w: jnp.pad(w, ((0, 0), (0, _HP - w.shape[1])))
    cols = [_pad_heads(qa, N_ATTN_HEADS, ATTN_HEAD_DIM), _pad_heads(ka, N_KV_HEADS, ATTN_HEAD_DIM),
            _pad_heads(qi, N_IDX_HEADS, IDX_HEAD_DIM), _pad_heads(ki, 1, IDX_HEAD_DIM),
            qg, kg, vg, go, pad_to(gk)]
    w = jnp.concatenate(cols, axis=1).astype(BF16)
    assert w.shape == (d, W_IN_PACKED)
    w_t = jnp.concatenate([_pad_heads(va, N_KV_HEADS, ATTN_HEAD_DIM),
                           jnp.pad(wi, ((0, 0), (0, SUBLANE - N_IDX_HEADS)))], axis=1).T.astype(BF16)
    assert w_t.shape == (W_IN_T, d)
    return w, w_t


def _inproj_kernel(x_ref, sc_ref, sh_ref, n1_ref, w_ref, wt_ref, cos_ref, s1_ref, s2_ref, qn_ref, kn_ref,
                   qa_ref, ka_ref, vt_ref, qi_ref, ki_ref, wit_ref,
                   qg_ref, kg_ref, vg_ref, gk_ref, go_ref):
    x = x_ref[...]
    ms = jnp.mean(x * x, axis=-1, keepdims=True)
    h = x * lax.rsqrt(ms + NORM_EPS) * n1_ref[...]
    h = h * (1.0 + sc_ref[0]) + sh_ref[0]
    hb = h.astype(BF16)
    cos = cos_ref[...]
    s1 = s1_ref[...]
    s2 = s2_ref[...]

    def proj(name):
        a, b = _OFF[name]
        return jnp.dot(hb, w_ref[:, a:b], preferred_element_type=F32)

    def rope(v):
        return v * cos + pltpu.roll(v, LANE - 8, 1) * s1 + pltpu.roll(v, 8, 1) * s2

    def head_norm(v, w):
        ssq = jnp.sum(v * v, axis=-1, keepdims=True)
        return v * lax.rsqrt(ssq * (1.0 / ATTN_HEAD_DIM) + NORM_EPS) * w

    p = proj("qa")
    for hh in range(N_ATTN_HEADS):
        sl = slice(hh * _HP, (hh + 1) * _HP)
        qa_ref[:, sl] = (rope(head_norm(p[:, sl], qn_ref[...])) * (ATTN_HEAD_DIM ** -0.5)).astype(BF16)
    p = proj("ka")
    for hh in range(N_KV_HEADS):
        sl = slice(hh * _HP, (hh + 1) * _HP)
        ka_ref[:, sl] = rope(head_norm(p[:, sl], kn_ref[...])).astype(BF16)
    pt = lax.dot_general(wt_ref[...], hb, NT_DIMS, preferred_element_type=F32)
    vt_ref[0] = pt[:W_KA].astype(BF16)
    wit_ref[...] = pt[W_KA:] * ((N_IDX_HEADS * IDX_HEAD_DIM) ** -0.5)
    p = proj("qi")
    for hh in range(N_IDX_HEADS):
        sl = slice(hh * _HP, (hh + 1) * _HP)
        qi_ref[:, sl] = rope(p[:, sl]).astype(BF16)
    ki_ref[...] = rope(proj("ki")).astype(BF16)
    p = proj("gqk")
    qg_ref[...] = p[:, :W_GQ]
    kg_ref[...] = p[:, W_GQ:]
    vg_ref[...] = proj("gv")
    go_ref[...] = proj("go")
    gk_ref[...] = proj("gk")


def _inproj(x2, sc1, sh1, n1w, w_packed, w_t, cos, s1, s2, qnw, knw, seq):
    t, d = x2.shape
    tm = min(_DSA_KEY_CHUNK, seq)
    per_b = seq // tm
    row = lambda w: pl.BlockSpec((tm, w), lambda i: (i, 0))
    mod = pl.BlockSpec((1, 1, d), lambda i: (i // per_b, 0, 0))
    one = lambda w: pl.BlockSpec((1, w), lambda i: (0, 0))
    sds = jax.ShapeDtypeStruct
    outs = (("qa", row(W_QA), (t, W_QA), BF16), ("ka", row(W_KA), (t, W_KA), BF16),
            ("vt", pl.BlockSpec((1, W_KA, tm), lambda i: (i, 0, 0)), (t // tm, W_KA, tm), BF16),
            ("qi", row(W_QI), (t, W_QI), BF16), ("ki", row(_HP), (t, _HP), BF16),
            ("wit", pl.BlockSpec((SUBLANE, tm), lambda i: (0, i)), (SUBLANE, t), F32),
            ("qg", row(W_GQ), (t, W_GQ), F32), ("kg", row(W_GQ), (t, W_GQ), F32),
            ("vg", row(W_GV), (t, W_GV), F32), ("gk", row(_HP), (t, _HP), F32),
            ("go", row(W_GV), (t, W_GV), F32))
    return pl.pallas_call(
        _inproj_kernel,
        grid=(t // tm,),
        in_specs=[row(d), mod, mod, one(d),
                  pl.BlockSpec((d, W_IN_PACKED), lambda i: (0, 0)),
                  pl.BlockSpec((W_IN_T, d), lambda i: (0, 0)),
                  row(LANE), row(LANE), row(LANE), one(LANE), one(LANE)],
        out_specs=[o[1] for o in outs],
        out_shape=[sds(o[2], o[3]) for o in outs],
        compiler_params=_cparams("arbitrary"),
        name="inproj",
    )(x2, sc1, sh1, n1w, w_packed, w_t, cos, s1, s2, qnw, knw)


_BISECT_STEPS = 6
_DSA_QUERY_TILE = 256
_BISECT_MAX_ROUNDS = 96


_FOLD_ROWS = 8 * SUBLANE


def _fold_keys(v, op):
    return op(v.reshape(v.shape[0] // _FOLD_ROWS, _FOLD_ROWS, v.shape[1]), axis=0)


def _dsa_kernel(qa_ref, ka_ref, vt_ref, qi_ref, ki_ref, wit_ref, nw_ref, o_ref, sc_ref, jstar_ref,
                *, n_sel, tq, ch, s_len):
    q0 = pl.program_id(1) * tq
    n_chunks = (q0 + tq + ch - 1) // ch
    wit = wit_ref[...]
    w_rows = [wit[hh:hh + 1, :] for hh in range(N_IDX_HEADS)]
    qcol = q0 + lax.broadcasted_iota(I32, (ch, tq), 1)
    krow = lax.broadcasted_iota(I32, (ch, tq), 0)
    neg = jnp.full((_FOLD_ROWS, tq), -jnp.inf, F32)
    pos = jnp.full((_FOLD_ROWS, tq), jnp.inf, F32)
    zero = jnp.zeros((_FOLD_ROWS, tq), F32)
    col_sum = lambda v: jnp.sum(v, axis=0, keepdims=True)
    col_max = lambda v: jnp.max(v, axis=0, keepdims=True)
    col_min = lambda v: jnp.min(v, axis=0, keepdims=True)

    def score_chunk(c, carry):
        top, bot = carry
        k0 = pl.multiple_of(c * ch, ch)
        kc = ki_ref[pl.ds(k0, ch), :]
        sc = None
        for hh in range(N_IDX_HEADS):
            d = lax.dot_general(kc, qi_ref[:, hh * _HP:(hh + 1) * _HP], NT_DIMS,
                                preferred_element_type=F32)
            term = w_rows[hh] * jnp.maximum(d, 0.0)
            sc = term if sc is None else sc + term
        causal = (k0 + krow) <= qcol
        sc_ref[c] = jnp.where(causal, sc, -jnp.inf)
        top = jnp.maximum(top, _fold_keys(jnp.where(causal, sc, -jnp.inf), jnp.max))
        bot = jnp.minimum(bot, _fold_keys(jnp.where(causal, sc, jnp.inf), jnp.min))
        return top, bot

    top, bot = lax.fori_loop(0, n_chunks, score_chunk, (neg, pos))

    n_causal = (q0 + 1 + lax.broadcasted_iota(I32, (1, tq), 1)).astype(F32)
    kk = jnp.minimum(n_causal, float(n_sel))
    lo0 = col_min(bot)
    hi0 = col_max(top)
    hi0 = jnp.where(n_causal <= float(n_sel), lo0, hi0)

    def count_above(piv):
        def body(c, acc):
            return acc + _fold_keys(jnp.where(sc_ref[c] > piv, 1.0, 0.0), jnp.sum)
        return col_sum(lax.fori_loop(0, n_chunks, body, zero))

    def search_round(carry):
        lo, hi, _, it = carry
        for _ in range(_BISECT_STEPS):
            mid = lo + 0.5 * (hi - lo)
            stuck = jnp.logical_or(mid <= lo, mid >= hi)
            piv = jnp.where(stuck, lo, mid)
            below = count_above(piv) < kk
            hi_n = jnp.where(below, piv, hi)
            lo = jnp.where(below, lo, jnp.where(stuck, hi, piv))
            hi = hi_n

        def tighten(c, carry):
            t_acc, b_acc = carry
            s = sc_ref[c]
            t_acc = jnp.maximum(t_acc, _fold_keys(jnp.where(s <= hi, s, -jnp.inf), jnp.max))
            b_acc = jnp.minimum(b_acc, _fold_keys(jnp.where(s >= lo, s, jnp.inf), jnp.min))
            return t_acc, b_acc

        t_acc, b_acc = lax.fori_loop(0, n_chunks, tighten, (neg, pos))
        top_v = col_max(t_acc)
        bot_v = col_min(b_acc)
        open_cols = jnp.max(jnp.where(top_v > bot_v, 1, 0))
        return bot_v, top_v, open_cols, it + 1

    def search_open(carry):
        return jnp.logical_and(carry[2] > 0, carry[3] < _BISECT_MAX_ROUNDS)

    open0 = jnp.max(jnp.where(hi0 > lo0, 1, 0))
    _, theta, _, _ = lax.while_loop(search_open, search_round, (lo0, hi0, open0, jnp.int32(0)))

    def tally(c, carry):
        a_acc, e_acc = carry
        s = sc_ref[c]
        return (a_acc + _fold_keys(jnp.where(s > theta, 1.0, 0.0), jnp.sum),
                e_acc + _fold_keys(jnp.where(s == theta, 1.0, 0.0), jnp.sum))

    a_acc, e_acc = lax.fori_loop(0, n_chunks, tally, (zero, zero))
    need = kk - col_sum(a_acc)
    n_tied = col_sum(e_acc)
    jstar_ref[...] = jnp.full(jstar_ref.shape, s_len, I32)

    @pl.when(jnp.max(jnp.where(n_tied > need, 1, 0)) > 0)
    def _():
        def step(_, bounds):
            lo_i, hi_i = bounds
            mid = (lo_i + hi_i) >> 1

            def body(c, acc):
                hit = jnp.where(sc_ref[c] == theta, jnp.where(c * ch + krow <= mid, 1.0, 0.0), 0.0)
                return acc + _fold_keys(hit, jnp.sum)

            ok = col_sum(lax.fori_loop(0, n_chunks, body, zero)) >= need
            return jnp.where(ok, lo_i, mid), jnp.where(ok, mid, hi_i)

        n_steps = int(np.ceil(np.log2(s_len))) + 1
        _, hi_i = lax.fori_loop(0, n_steps, step,
                                (jnp.full((1, tq), -1, I32), jnp.full((1, tq), s_len - 1, I32)))
        jstar_ref[...] = jnp.broadcast_to(hi_i, jstar_ref.shape)

    jstar = jstar_ref[0:1, :]

    def to_bias(c, carry):
        s = sc_ref[c]
        sc_ref[c] = jnp.where(s == theta, jnp.where(c * ch + krow <= jstar, 0.0, -jnp.inf),
                              jnp.where(s > theta, 0.0, -jnp.inf))
        return carry

    lax.fori_loop(0, n_chunks, to_bias, 0)

    rep = N_ATTN_HEADS // N_KV_HEADS
    for g in range(N_KV_HEADS):
        gs = slice(g * _HP, (g + 1) * _HP)

        def att_chunk(c, carry):
            k0 = pl.multiple_of(c * ch, ch)
            kc = ka_ref[pl.ds(k0, ch), gs]
            vt = vt_ref[c, gs, :]
            bias = sc_ref[c]
            out = []
            for r in range(rep):
                m, den, acc = carry[r]
                hh = g * rep + r
                s = lax.dot_general(kc, qa_ref[:, hh * _HP:(hh + 1) * _HP], NT_DIMS,
                                    preferred_element_type=F32) + bias
                m_new = jnp.maximum(m, col_max(_fold_keys(s, jnp.max)))
                m_use = jnp.where(m_new == -jnp.inf, 0.0, m_new)
                alpha = jnp.exp(m - m_use)
                p = jnp.exp(s - m_use)
                den = alpha * den + col_sum(_fold_keys(p, jnp.sum))
                acc = alpha * acc + jnp.dot(vt, p.astype(BF16), preferred_element_type=F32)
                out.append((m_new, den, acc))
            return tuple(out)

        init = tuple((jnp.full((1, tq), -jnp.inf, F32), jnp.zeros((1, tq), F32),
                      jnp.zeros((_HP, tq), F32)) for _ in range(rep))
        res = lax.fori_loop(0, n_chunks, att_chunk, init)
        for r in range(rep):
            hh = g * rep + r
            _, den, acc = res[r]
            o = acc / den
            ssq = col_sum(o * o)
            on = (o * lax.rsqrt(ssq * (1.0 / ATTN_HEAD_DIM) + NORM_EPS)).T * nw_ref[hh:hh + 1, :]
            o_ref[:, hh * _HP:(hh + 1) * _HP] = on.astype(BF16)


def _dsa(qa, ka, vt, qi, ki, wit, nw, bsz, seq):
    t = qa.shape[0]
    tq = min(_DSA_QUERY_TILE, seq)
    ch = min(_DSA_KEY_CHUNK, seq)
    nq = seq // tq
    nc = seq // ch
    n_sel = min(TOPK_KEYS, seq // 4)
    qrow = lambda w: pl.BlockSpec((tq, w), lambda b, j: (b * nq + j, 0))
    krow = lambda w: pl.BlockSpec((seq, w), lambda b, j: (b, 0))
    return pl.pallas_call(
        functools.partial(_dsa_kernel, n_sel=n_sel, tq=tq, ch=ch, s_len=seq),
        grid=(bsz, nq),
        in_specs=[qrow(W_QA), krow(W_KA), pl.BlockSpec((nc, W_KA, ch), lambda b, j: (b, 0, 0)),
                  qrow(W_QI), krow(_HP), pl.BlockSpec((SUBLANE, tq), lambda b, j: (0, b * nq + j)),
                  pl.BlockSpec((N_ATTN_HEADS, _HP), lambda b, j: (0, 0))],
        out_specs=qrow(W_QA),
        out_shape=jax.ShapeDtypeStruct((t, W_QA), BF16),
        scratch_shapes=[pltpu.VMEM((nc, ch, tq), F32), pltpu.VMEM((SUBLANE, tq), I32)],
        compiler_params=_cparams("arbitrary", "arbitrary"),
        name="dsa",
    )(qa, ka, vt, qi, ki, wit, nw)


_GLA_LEVELS = 6
_GLA_ROWS = 256


def _gla_consts():
    c = GLA_CHUNK
    hk = N_GLA_HEADS * GLA_KEY_DIM
    i = np.arange(c)
    cum = [np.tril(np.ones((c, c)))]
    rq, pm = [], []
    t = i[:, None]
    s = (np.arange(hk) % GLA_KEY_DIM)[None, :]
    pm.append((t == s).astype(np.float32))
    for lvl in range(1, _GLA_LEVELS + 1):
        blk, half = 1 << lvl, 1 << (lvl - 1)
        split = (i // blk) * blk + half - 1
        cum.append((np.arange(c)[None, :] <= split[:, None]).astype(np.float64))
        right = ((i // half) % 2 == 1)
        rq.append(np.broadcast_to(right[:, None], (c, hk)).astype(np.float32))
        pm.append(((t // blk == s // blk) & ((t // half) % 2 == 1) & ((s // half) % 2 == 0))
                  .astype(np.float32))
    r = np.arange(hk)
    bd_k = (r[:, None] // GLA_KEY_DIM == r[None, :] // GLA_KEY_DIM).astype(np.float32)
    cv = np.arange(N_GLA_HEADS * GLA_VAL_DIM)
    bd_v = (r[:, None] // GLA_KEY_DIM == cv[None, :] // GLA_VAL_DIM).astype(np.float32)
    return (jnp.asarray(np.concatenate(cum, 0), BF16), jnp.asarray(np.stack(rq)),
            jnp.asarray(np.stack(pm)), jnp.asarray(bd_k), jnp.asarray(bd_v),
            jnp.asarray(bd_v.T.copy()))


def _gla_kernel(qg_ref, kg_ref, vg_ref, gk_ref, go_ref, wgk_ref, bgk_ref, gnw_ref,
                cum_ref, rq_ref, pm_ref, bdk_ref, bdv_ref, bdvt_ref, o_ref, st_ref):
    c = GLA_CHUNK

    @pl.when(pl.program_id(1) == 0)
    def _():
        st_ref[...] = jnp.zeros(st_ref.shape, F32)

    bdk = bdk_ref[...]
    cum = cum_ref[...]
    for ci in range(_GLA_ROWS // c):
        sl = slice(ci * c, (ci + 1) * c)
        q = qg_ref[sl, :] * (GLA_KEY_DIM ** -0.5)
        k = kg_ref[sl, :]
        v = vg_ref[sl, :]
        z = jnp.dot(gk_ref[sl, :].astype(BF16), wgk_ref[...], preferred_element_type=F32) + bgk_ref[...]
        la = (jnp.minimum(z, 0.0) - jnp.log1p(jnp.exp(-jnp.abs(z)))) * (1.0 / GLA_GATE_TAU)
        la_hi = la.astype(BF16)
        la_lo = (la - la_hi.astype(F32)).astype(BF16)
        cb = (jnp.dot(cum, la_hi, preferred_element_type=F32)
              + jnp.dot(cum, la_lo, preferred_element_type=F32))
        b = cb[0:c]
        kbd = (jnp.concatenate([k] * N_GLA_HEADS, axis=0) * bdk).astype(BF16)
        a = lax.dot_general(q.astype(BF16), kbd, NT_DIMS, preferred_element_type=F32) * pm_ref[0]
        for lvl in range(1, _GLA_LEVELS + 1):
            beta = cb[lvl * c:(lvl + 1) * c]
            rq = rq_ref[lvl - 1]
            qs = q * jnp.exp(jnp.minimum(b - beta, 0.0)) * rq
            ks = k * jnp.exp(jnp.minimum(beta - b, 0.0)) * (1.0 - rq)
            ksbd = (jnp.concatenate([ks] * N_GLA_HEADS, axis=0) * bdk).astype(BF16)
            a = a + lax.dot_general(qs.astype(BF16), ksbd, NT_DIMS,
                                    preferred_element_type=F32) * pm_ref[lvl]
        vbd = (jnp.concatenate([v] * N_GLA_HEADS, axis=0) * bdv_ref[...]).astype(BF16)
        intra = jnp.dot(a.astype(BF16), vbd, preferred_element_type=F32)
        st = st_ref[...]
        inter = lax.dot_general((q * jnp.exp(b)).astype(BF16), st.astype(BF16), NT_DIMS,
                                preferred_element_type=F32)
        o = intra + inter
        b_end = b[c - 1:c, :]
        k_dec = (k * jnp.exp(b_end - b)).astype(BF16)
        upd = lax.dot_general(v.astype(BF16), k_dec, TN_DIMS, preferred_element_type=F32)
        st_ref[...] = st * jnp.exp(b_end) + upd * bdvt_ref[...]
        for hh in range(N_GLA_HEADS):
            hs = slice(hh * GLA_VAL_DIM, (hh + 1) * GLA_VAL_DIM)
            oh = o[:, hs]
            ms = jnp.mean(oh * oh, axis=-1, keepdims=True)
            on = oh * lax.rsqrt(ms + NORM_EPS) * gnw_ref[:, hs]
            o_ref[sl, hs] = (on * _silu(go_ref[sl, hs])).astype(BF16)


def _gla(qg, kg, vg, gk, go, wgk, bgk, gnw, bsz, seq):
    t = qg.shape[0]
    rows = _GLA_ROWS
    assert seq % rows == 0
    nj = seq // rows
    consts = _gla_consts()
    row = lambda w: pl.BlockSpec((rows, w), lambda b, j: (b * nj + j, 0))
    full = lambda a: pl.BlockSpec(a.shape, lambda b, j: (0,) * a.ndim)
    args = (qg, kg, vg, gk, go, wgk, bgk, gnw) + consts
    return pl.pallas_call(
        _gla_kernel,
        grid=(bsz, nj),
        in_specs=[row(W_GQ), row(W_GQ), row(W_GV), row(_HP), row(W_GV)] + [full(a) for a in args[5:]],
        out_specs=row(W_GV),
        out_shape=jax.ShapeDtypeStruct((t, W_GV), BF16),
        scratch_shapes=[pltpu.VMEM((W_GV, W_GQ), F32)],
        compiler_params=_cparams("arbitrary", "arbitrary"),
        name="gla",
    )(*args)


def _outproj_kernel(attn_ref, gla_ref, x_ref, g1_ref, sc_ref, sh_ref, n2_ref, wa_ref, wb_ref,
                    wr_ref, br_ref, tri_ref,
                    x1_ref, h2_ref, idx_ref, wsel_ref, rank_ref, ccol_ref, crow_ref, *, tm):
    @pl.when(pl.program_id(0) == 0)
    def _():
        ccol_ref[...] = jnp.zeros(ccol_ref.shape, F32)
        crow_ref[...] = jnp.zeros(crow_ref.shape, F32)

    mixed = (jnp.dot(attn_ref[...], wa_ref[...], preferred_element_type=F32)
             + jnp.dot(gla_ref[...], wb_ref[...], preferred_element_type=F32))
    x1 = x_ref[...] + g1_ref[0] * mixed
    x1_ref[...] = x1
    ms = jnp.mean(x1 * x1, axis=-1, keepdims=True)
    h2 = x1 * lax.rsqrt(ms + NORM_EPS) * n2_ref[...]
    h2 = h2 * (1.0 + sc_ref[0]) + sh_ref[0]
    _store_slabs(h2_ref, 0, tm, _pack_row_halves(h2))

    logits = lax.dot_general(wr_ref[...], h2.astype(BF16), NT_DIMS, preferred_element_type=F32)
    scores = jax.nn.sigmoid(logits)
    biased = scores + jnp.concatenate([br_ref[...]] * (tm // LANE), axis=1)
    per_group = N_EXPERTS // N_EXPERT_GROUPS
    gs = []
    for g in range(N_EXPERT_GROUPS):
        blk = biased[g * per_group:(g + 1) * per_group]
        m1 = jnp.max(blk, axis=0, keepdims=True)
        n1 = jnp.sum(jnp.where(blk == m1, 1.0, 0.0), axis=0, keepdims=True)
        m2 = jnp.max(jnp.where(blk < m1, blk, -jnp.inf), axis=0, keepdims=True)
        gs.append(m1 + jnp.where(n1 >= 2.0, m1, m2))
    gscore = jnp.concatenate(gs, axis=0)
    gi = lax.broadcasted_iota(I32, gscore.shape, 0)
    gsel = jnp.zeros(gscore.shape, F32)
    for _ in range(TOPK_GROUPS):
        m = jnp.max(gscore, axis=0, keepdims=True)
        first = jnp.min(jnp.where(gscore == m, gi, N_EXPERT_GROUPS), axis=0, keepdims=True)
        hit = gi == first
        gsel = jnp.where(hit, 1.0, gsel)
        gscore = jnp.where(hit, -jnp.inf, gscore)
    gmask = jnp.concatenate(
        [jnp.broadcast_to(gsel[g:g + 1], (per_group, tm)) for g in range(N_EXPERT_GROUPS)], axis=0)
    masked = jnp.where(gmask > 0.0, biased, -jnp.inf)
    ei = lax.broadcasted_iota(I32, masked.shape, 0)
    chosen = jnp.zeros(masked.shape, F32)
    idxs, ws = [], []
    for _ in range(TOPK_EXPERTS):
        m = jnp.max(masked, axis=0, keepdims=True)
        first = jnp.min(jnp.where(masked == m, ei, N_EXPERTS), axis=0, keepdims=True)
        hit = ei == first
        idxs.append(first)
        ws.append(jnp.sum(jnp.where(hit, scores, 0.0), axis=0, keepdims=True))
        chosen = jnp.where(hit, 1.0, chosen)
        masked = jnp.where(hit, -jnp.inf, masked)
    wsum = ws[0]
    for w in ws[1:]:
        wsum = wsum + w
    idx_ref[...] = jnp.concatenate(idxs, axis=0)
    wsel_ref[...] = jnp.concatenate([w / wsum * ROUTED_SCALE for w in ws], axis=0)

    chosen_b = chosen.astype(BF16)
    before = jnp.dot(chosen_b, tri_ref[...], preferred_element_type=F32)
    seen = ccol_ref[...]
    pos = before + jnp.concatenate([seen] * (tm // LANE), axis=1)
    rank_ref[...] = jnp.concatenate(
        [jnp.sum(jnp.where(ei == i, pos, 0.0), axis=0, keepdims=True) for i in idxs],
        axis=0).astype(I32)
    ccol_ref[...] = seen + jnp.sum(chosen, axis=1, keepdims=True)
    crow_ref[...] = crow_ref[...] + lax.dot_general(
        jnp.ones((SUBLANE, tm), BF16), chosen_b, NT_DIMS, preferred_element_type=F32)


def _outproj(attn, gla, x2, g1, sc2, sh2, n2w, wa, wb, wr_t, br, seq):
    t, d = x2.shape
    tm = min(512, seq)
    per_b = seq // tm
    row = lambda w: pl.BlockSpec((tm, w), lambda i: (i, 0))
    col = lambda: pl.BlockSpec((TOPK_EXPERTS, tm), lambda i: (0, i))
    mod = pl.BlockSpec((1, 1, d), lambda i: (i // per_b, 0, 0))
    full = lambda a: pl.BlockSpec(a.shape, lambda i: (0,) * a.ndim)
    tri = jnp.asarray(np.triu(np.ones((tm, tm), np.float32), 1), BF16)
    br_col = jnp.broadcast_to(br.reshape(N_EXPERTS, 1).astype(F32), (N_EXPERTS, LANE))
    n2 = n2w.reshape(1, d)
    n_slabs = d // 2 // LANE
    sel = jax.ShapeDtypeStruct((TOPK_EXPERTS, t), I32)
    return pl.pallas_call(
        functools.partial(_outproj_kernel, tm=tm),
        grid=(t // tm,),
        in_specs=[row(W_QA), row(W_GV), row(d), mod, mod, mod, full(n2), full(wa), full(wb),
                  full(wr_t), full(br_col), full(tri)],
        out_specs=[row(d), pl.BlockSpec((tm * n_slabs, LANE), lambda i: (i, 0)), col(), col(), col(),
                   pl.BlockSpec((N_EXPERTS, LANE), lambda i: (0, 0)),
                   pl.BlockSpec((SUBLANE, N_EXPERTS), lambda i: (0, 0))],
        out_shape=[jax.ShapeDtypeStruct((t, d), F32), jax.ShapeDtypeStruct((t * n_slabs, LANE), U32),
                   sel, jax.ShapeDtypeStruct((TOPK_EXPERTS, t), F32), sel,
                   jax.ShapeDtypeStruct((N_EXPERTS, LANE), F32),
                   jax.ShapeDtypeStruct((SUBLANE, N_EXPERTS), F32)],
        compiler_params=_cparams("arbitrary"),
        name="outproj",
    )(attn, gla, x2, g1, sc2, sh2, n2, wa, wb, wr_t, br_col, tri)


def _plan_kernel(ccol_ref, crow_ref, idx_ref, rank_ref, dest_ref, blk_ref, nblk_ref, *, n_blocks):
    m = float(EXPERT_ROWS)
    pad_row = jnp.floor((crow_ref[0:1, :] + (m - 1.0)) * (1.0 / m)) * m
    pad_col = jnp.floor((ccol_ref[:, 0:1] + (m - 1.0)) * (1.0 / m)) * m
    er = lax.broadcasted_iota(I32, (N_EXPERTS, N_EXPERTS), 0)
    ec = lax.broadcasted_iota(I32, (N_EXPERTS, N_EXPERTS), 1)
    start_col = jnp.sum(jnp.where(ec < er, pad_row, 0.0), axis=1, keepdims=True)
    end_col = start_col + pad_col
    first_row = (lax.broadcasted_iota(I32, (N_EXPERTS, n_blocks), 1) * EXPERT_ROWS).astype(F32)
    owner = jnp.sum(jnp.where(end_col <= first_row, 1.0, 0.0), axis=0, keepdims=True)
    blk_ref[...] = jnp.minimum(owner, float(N_EXPERTS - 1)).astype(I32)
    total = jnp.sum(pad_row, axis=1, keepdims=True)
    nblk_ref[...] = jnp.broadcast_to(total * (1.0 / m), nblk_ref.shape).astype(I32)
    tm = idx_ref.shape[1]
    ei = lax.broadcasted_iota(I32, (N_EXPERTS, tm), 0)
    rows = []
    for kc in range(TOPK_EXPERTS):
        hit = ei == idx_ref[kc:kc + 1, :]
        rows.append(jnp.sum(jnp.where(hit, start_col, 0.0), axis=0, keepdims=True))
    dest_ref[...] = jnp.concatenate(rows, axis=0).astype(I32) + rank_ref[...]


def _plan(ccol, crow, idx, rank, n_blocks):
    t = idx.shape[1]
    tm = min(1024, t)
    col = pl.BlockSpec((TOPK_EXPERTS, tm), lambda i: (0, i))
    full = lambda a: pl.BlockSpec(a.shape, lambda i: (0,) * a.ndim)
    return pl.pallas_call(
        functools.partial(_plan_kernel, n_blocks=n_blocks),
        grid=(t // tm,),
        in_specs=[full(ccol), full(crow), col, col],
        out_specs=[col, pl.BlockSpec((1, n_blocks), lambda i: (0, 0)),
                   pl.BlockSpec((1, LANE), lambda i: (0, 0))],
        out_shape=[jax.ShapeDtypeStruct((TOPK_EXPERTS, t), I32),
                   jax.ShapeDtypeStruct((1, n_blocks), I32),
                   jax.ShapeDtypeStruct((1, LANE), I32)],
        compiler_params=_cparams("arbitrary"),
        name="plan",
    )(ccol, crow, idx, rank)


def _row_copy(src, src_row, dst, dst_row, sem, n_slabs):
    return pltpu.make_async_copy(src.at[pl.ds(src_row * n_slabs, n_slabs), :],
                                 dst.at[pl.ds(dst_row * n_slabs, n_slabs), :], sem)


def _scatter_kernel(dest_ref, h_ref, xb_in_ref, xb_ref, sem, *, tm, n_slabs):
    del xb_in_ref

    def issue(tok, carry):
        for kc in range(TOPK_EXPERTS):
            _row_copy(h_ref, tok, xb_ref, dest_ref[kc, tok], sem, n_slabs).start()
        return carry

    lax.fori_loop(0, tm, issue, 0)

    def drain(tok, carry):
        for kc in range(TOPK_EXPERTS):
            _row_copy(h_ref, tok, xb_ref, dest_ref[kc, tok], sem, n_slabs).wait()
        return carry

    lax.fori_loop(0, tm, drain, 0)


def _scatter(dest, h_rows, xb_zero, n_slabs):
    t = dest.shape[1]
    tm = min(128, t)
    return pl.pallas_call(
        functools.partial(_scatter_kernel, tm=tm, n_slabs=n_slabs),
        grid=(t // tm,),
        in_specs=[pl.BlockSpec((TOPK_EXPERTS, tm), lambda i: (0, i), memory_space=pltpu.SMEM),
                  pl.BlockSpec((tm * n_slabs, LANE), lambda i: (i, 0)),
                  pl.BlockSpec(memory_space=pl.ANY)],
        out_specs=pl.BlockSpec(memory_space=pl.ANY),
        out_shape=jax.ShapeDtypeStruct(xb_zero.shape, U32),
        scratch_shapes=[pltpu.SemaphoreType.DMA],
        input_output_aliases={2: 0},
        compiler_params=_cparams("arbitrary"),
        name="scatter",
    )(dest, h_rows, xb_zero)


def _experts_kernel(blk_ref, nblk_ref, x_ref, wg_ref, wu_ref, wd_ref, y_ref, wgb, wub, wdb, *, n_slabs):
    i = pl.program_id(0)
    m = EXPERT_ROWS

    @pl.when(i >= nblk_ref[0])
    def _():
        y_ref[...] = jnp.zeros(y_ref.shape, U32)

    @pl.when(i < nblk_ref[0])
    def _():
        prev = blk_ref[jnp.maximum(i - 1, 0)]

        @pl.when(jnp.logical_or(i == 0, blk_ref[i] != prev))
        def _():
            wgb[...] = wg_ref[0].astype(BF16)
            wub[...] = wu_ref[0].astype(BF16)
            wdb[...] = wd_ref[0].astype(BF16)

        x = _unpack_row_halves(_load_slabs(x_ref, 0, m, n_slabs)).astype(BF16)
        g = jnp.dot(x, wgb[...], preferred_element_type=F32)
        u = jnp.dot(x, wub[...], preferred_element_type=F32)
        y = jnp.dot((_silu(g) * u).astype(BF16), wdb[...], preferred_element_type=F32)
        _store_slabs(y_ref, 0, m, _pack_row_halves(y))


def _experts(blk, nblk, xb, we_gate, we_up, we_down, n_blocks):
    m = EXPERT_ROWS
    e, d, f = we_gate.shape
    n_slabs = d // 2 // LANE
    live = lambda i, blk, nblk: jnp.minimum(i, jnp.maximum(nblk[0] - 1, 0))
    grid_spec = pltpu.PrefetchScalarGridSpec(
        num_scalar_prefetch=2,
        grid=(n_blocks,),
        in_specs=[pl.BlockSpec((m * n_slabs, LANE), lambda i, blk, nblk: (live(i, blk, nblk), 0)),
                  pl.BlockSpec((1, d, f), lambda i, blk, nblk: (blk[live(i, blk, nblk)], 0, 0)),
                  pl.BlockSpec((1, d, f), lambda i, blk, nblk: (blk[live(i, blk, nblk)], 0, 0)),
                  pl.BlockSpec((1, f, d), lambda i, blk, nblk: (blk[live(i, blk, nblk)], 0, 0))],
        out_specs=pl.BlockSpec((m * n_slabs, LANE), lambda i, blk, nblk: (i, 0)),
        scratch_shapes=[pltpu.VMEM((d, f), BF16), pltpu.VMEM((d, f), BF16), pltpu.VMEM((f, d), BF16)],
    )
    return pl.pallas_call(
        functools.partial(_experts_kernel, n_slabs=n_slabs),
        grid_spec=grid_spec,
        out_shape=jax.ShapeDtypeStruct(xb.shape, U32),
        compiler_params=_cparams("arbitrary"),
        name="experts",
    )(blk, nblk, xb, we_gate, we_up, we_down)


def _combine_kernel(dest_ref, yb_ref, w_ref, h_ref, x1_ref, g2_ref, wsg_ref, wsu_ref, wsd_ref,
                    o_ref, buf, sem, *, tm, n_slabs):
    def issue(tok, carry):
        for kc in range(TOPK_EXPERTS):
            _row_copy(yb_ref, dest_ref[kc, tok], buf, kc * tm + tok, sem, n_slabs).start()
        return carry

    lax.fori_loop(0, tm, issue, 0)

    hb = _unpack_row_halves(_load_slabs(h_ref, 0, tm, n_slabs)).astype(BF16)
    g = jnp.dot(hb, wsg_ref[...], preferred_element_type=F32)
    u = jnp.dot(hb, wsu_ref[...], preferred_element_type=F32)
    shared = jnp.dot((_silu(g) * u).astype(BF16), wsd_ref[...], preferred_element_type=F32)

    def drain(tok, carry):
        for kc in range(TOPK_EXPERTS):
            _row_copy(yb_ref, dest_ref[kc, tok], buf, kc * tm + tok, sem, n_slabs).wait()
        return carry

    lax.fori_loop(0, tm, drain, 0)

    w = w_ref[...]
    routed = None
    for kc in range(TOPK_EXPERTS):
        term = w[:, kc:kc + 1] * _unpack_row_halves(_load_slabs(buf, kc * tm, tm, n_slabs))
        routed = term if routed is None else routed + term
    o_ref[...] = x1_ref[...] + g2_ref[0] * (routed + shared)


def _combine(dest, yb, w_tok, h_rows, x1, g2, wsg, wsu, wsd, seq):
    t, d = x1.shape
    tm = min(128, seq)
    per_b = seq // tm
    n_slabs = d // 2 // LANE
    row = lambda w: pl.BlockSpec((tm, w), lambda i: (i, 0))
    full = lambda a: pl.BlockSpec(a.shape, lambda i: (0,) * a.ndim)
    return pl.pallas_call(
        functools.partial(_combine_kernel, tm=tm, n_slabs=n_slabs),
        grid=(t // tm,),
        in_specs=[pl.BlockSpec((TOPK_EXPERTS, tm), lambda i: (0, i), memory_space=pltpu.SMEM),
                  pl.BlockSpec(memory_space=pl.ANY),
                  row(TOPK_EXPERTS), pl.BlockSpec((tm * n_slabs, LANE), lambda i: (i, 0)), row(d),
                  pl.BlockSpec((1, 1, d), lambda i: (i // per_b, 0, 0)),
                  full(wsg), full(wsu), full(wsd)],
        out_specs=row(d),
        out_shape=jax.ShapeDtypeStruct((t, d), F32),
        scratch_shapes=[pltpu.VMEM((TOPK_EXPERTS * tm * n_slabs, LANE), U32),
                        pltpu.SemaphoreType.DMA],
        compiler_params=_cparams("arbitrary"),
        name="combine",
    )(dest, yb, w_tok, h_rows, x1, g2, wsg, wsu, wsd)


def _layer(x2, cond_mod, positions_tables, p, bsz, seq):
    t, d = x2.shape
    cos, s1, s2 = positions_tables
    sh1, sc1, g1, sh2, sc2, g2 = [m.reshape(bsz, 1, d) for m in jnp.split(cond_mod, 6, axis=-1)]

    lane_pad = lambda w: jnp.pad(w.astype(F32), ((0, 0), (0, _HP - w.shape[-1])))
    qnw = lane_pad(p["q_norm_w"].reshape(1, ATTN_HEAD_DIM))
    knw = lane_pad(p["k_norm_w"].reshape(1, ATTN_HEAD_DIM))
    qa, ka, vt, qi, ki, wit, qg, kg, vg, gk, go = _inproj(
        x2, sc1, sh1, p["norm1_w"].reshape(1, d), *_pack_w_in(p["w_in"]), cos, s1, s2, qnw, knw, seq)

    attn = _dsa(qa, ka, vt, qi, ki, wit, lane_pad(p["attn_out_norm_w"]), bsz, seq)

    wgk = jnp.pad(p["w_gk2"], ((0, _HP - GLA_GATE_RANK), (0, 0))).astype(BF16)
    gla = _gla(qg, kg, vg, gk, go, wgk, p["b_gk"].reshape(1, W_GQ).astype(F32),
               p["gla_norm_w"].reshape(1, W_GV).astype(F32), bsz, seq)

    w_out = p["w_out"]
    n_attn = N_ATTN_HEADS * ATTN_HEAD_DIM
    wa = jnp.pad(w_out[:n_attn].reshape(N_ATTN_HEADS, ATTN_HEAD_DIM, d),
                 ((0, 0), (0, _HP - ATTN_HEAD_DIM), (0, 0))).reshape(W_QA, d).astype(BF16)
    wb = w_out[n_attn:].astype(BF16)
    x1, h2, idx, wsel, rank, ccol, crow = _outproj(
        attn, gla, x2, g1, sc2, sh2, p["norm2_w"], wa, wb,
        p["w_router"].T.astype(BF16), p["b_router"], seq)

    n_blocks = (t * TOPK_EXPERTS) // EXPERT_ROWS + N_EXPERTS
    dest, blk, nblk = _plan(ccol, crow, idx, rank, n_blocks)

    n_rows = n_blocks * EXPERT_ROWS
    n_slabs = d // 2 // LANE
    xb = _scatter(dest, h2, jnp.zeros((n_rows * n_slabs, LANE), U32), n_slabs)
    yb = _experts(blk.reshape(n_blocks), nblk.reshape(LANE)[:1], xb,
                  p["we_gate"], p["we_up"], p["we_down"], n_blocks)
    return _combine(dest, yb, wsel.T, h2, x1, g2, p["ws_gate"].astype(BF16),
                    p["ws_up"].astype(BF16), p["ws_down"].astype(BF16), seq)


def kernel(x, c, positions, norm1_w, norm2_w, w_ada, b_ada, w_in, q_norm_w, k_norm_w,
           attn_out_norm_w, w_gk2, b_gk, gla_norm_w, w_out, w_router, b_router,
           we_gate, we_up, we_down, ws_gate, ws_up, ws_down):
    bsz, seq, d = x.shape
    depth = w_ada.shape[0]
    x2 = x.reshape(bsz * seq, d)
    tables = _rope_tables(positions)
    stacked = dict(norm1_w=norm1_w, norm2_w=norm2_w, w_in=w_in, q_norm_w=q_norm_w, k_norm_w=k_norm_w,
                   attn_out_norm_w=attn_out_norm_w, w_gk2=w_gk2, b_gk=b_gk, gla_norm_w=gla_norm_w,
                   w_out=w_out, w_router=w_router, b_router=b_router, we_gate=we_gate, we_up=we_up,
                   we_down=we_down, ws_gate=ws_gate, ws_up=ws_up, ws_down=ws_down)
    for l in range(depth):
        mod = _ada(c.astype(F32), w_ada[l], b_ada[l])
        x2 = _layer(x2, mod, tables, {k: v[l] for k, v in stacked.items()}, bsz, seq)
    return x2.reshape(bsz, seq, d)
```

```python
import functools

import numpy as np
import jax
import jax.numpy as jnp
from jax import lax
from jax.experimental import pallas as pl
from jax.experimental.pallas import tpu as pltpu

F32 = jnp.float32
BF16 = jnp.bfloat16
I32 = jnp.int32

LANE = 128
SUBLANE = 8
VMEM_LIMIT = 48 * 1024 * 1024

N_ATTN_HEADS = 8
ATTN_HEAD_DIM = 64
N_KV_HEADS = 2
N_IDX_HEADS = 4
IDX_HEAD_DIM = 64
TOPK_KEYS = 256
N_GLA_HEADS = 4
GLA_KEY_DIM = 64
GLA_VAL_DIM = 128
GLA_GATE_RANK = 16
GLA_GATE_TAU = 16.0
GLA_CHUNK = 64
ROPE_THETA = 500000.0
ROPE_FRACTION = 4
N_EXPERTS = 256
N_EXPERT_GROUPS = 8
TOPK_GROUPS = 4
TOPK_EXPERTS = 8
EXPERT_DIM = 256
ROUTED_SCALE = 2.5
NORM_EPS = 1e-6
IN_SIZES = (512, 128, 128, 256, 64, 4, 256, 256, 512, 16, 512)

U32 = jnp.uint32
EXPERT_ROWS = 512
NT_DIMS = (((1,), (1,)), ((), ()))
TN_DIMS = (((0,), (0,)), ((), ()))


def _cparams(*sem):
    return pltpu.CompilerParams(dimension_semantics=sem, vmem_limit_bytes=VMEM_LIMIT)


def _silu(v):
    return v * jax.nn.sigmoid(v)


def _pack_row_halves(v):
    half = v.shape[1] // 2
    bits = lambda a: lax.bitcast_convert_type(a.astype(jnp.bfloat16).astype(F32), U32)
    return bits(v[:, half:]) | (bits(v[:, :half]) >> 16)


def _unpack_row_halves(w):
    lo = lax.bitcast_convert_type(w << 16, F32)
    hi = lax.bitcast_convert_type(w & jnp.uint32(0xFFFF0000), F32)
    return jnp.concatenate([lo, hi], axis=1)


def _store_slabs(ref, row0, n_rows, w):
    n_slabs = w.shape[1] // LANE
    for sb in range(n_slabs):
        ref[pl.ds(row0 * n_slabs + sb, n_rows, stride=n_slabs), :] = w[:, sb * LANE:(sb + 1) * LANE]


def _load_slabs(ref, row0, n_rows, n_slabs):
    return jnp.concatenate(
        [ref[pl.ds(row0 * n_slabs + sb, n_rows, stride=n_slabs), :] for sb in range(n_slabs)], axis=1)


def _ada_kernel(c_ref, w_ref, b_ref, o_ref):
    cond = _silu(c_ref[...])
    o_ref[...] = jnp.dot(cond.astype(BF16), w_ref[...].astype(BF16),
                         preferred_element_type=F32) + b_ref[...]


def _ada(c, w, b):
    bsz, d = c.shape
    n = w.shape[1]
    tn = 1536
    return pl.pallas_call(
        _ada_kernel,
        grid=(n // tn,),
        in_specs=[pl.BlockSpec((bsz, d), lambda i: (0, 0)),
                  pl.BlockSpec((d, tn), lambda i: (0, i)),
                  pl.BlockSpec((1, tn), lambda i: (0, i))],
        out_specs=pl.BlockSpec((bsz, tn), lambda i: (0, i)),
        out_shape=jax.ShapeDtypeStruct((bsz, n), F32),
        compiler_params=_cparams("arbitrary"),
        name="ada",
    )(c, w, b.reshape(1, n))


def _rope_kernel(pos_ref, invf_ref, m1_ref, m2_ref, cos_ref, s1_ref, s2_ref):
    ang = pos_ref[...] * invf_ref[...]
    s = jnp.sin(ang)
    cos_ref[...] = jnp.cos(ang)
    s1_ref[...] = -s * m1_ref[...]
    s2_ref[...] = s * m2_ref[...]


def _rope_tables(positions):
    t = positions.size
    rot = ATTN_HEAD_DIM // ROPE_FRACTION
    half = rot // 2
    inv_freq = jnp.power(jnp.float32(ROPE_THETA), -jnp.arange(half, dtype=F32) / half)
    lane = np.arange(LANE) % ATTN_HEAD_DIM
    invf = jnp.where(lane < rot, inv_freq[lane % half], 0.0).astype(F32).reshape(1, LANE)
    m1 = jnp.asarray((lane < half).astype(np.float32)).reshape(1, LANE)
    m2 = jnp.asarray(((lane >= half) & (lane < rot)).astype(np.float32)).reshape(1, LANE)
    pos = jnp.broadcast_to(positions.reshape(t, 1).astype(F32), (t, LANE))
    tm = min(2048, t)
    row = pl.BlockSpec((tm, LANE), lambda i: (i, 0))
    one = pl.BlockSpec((1, LANE), lambda i: (0, 0))
    sds = jax.ShapeDtypeStruct((t, LANE), F32)
    return pl.pallas_call(
        _rope_kernel,
        grid=(t // tm,),
        in_specs=[row, one, one, one],
        out_specs=[row, row, row],
        out_shape=[sds, sds, sds],
        compiler_params=_cparams("arbitrary"),
        name="rope",
    )(pos, invf, m1, m2)


_HP = LANE
W_QA = N_ATTN_HEADS * _HP
W_KA = N_KV_HEADS * _HP
W_QI = N_IDX_HEADS * _HP
W_GQ = N_GLA_HEADS * GLA_KEY_DIM
W_GV = N_GLA_HEADS * GLA_VAL_DIM
_GROUPS = (("qa", W_QA), ("ka", W_KA), ("qi", W_QI), ("ki", _HP),
           ("gqk", 2 * W_GQ), ("gv", W_GV), ("go", W_GV), ("gk", _HP))
_OFF = {}
_o = 0
for _n, _w in _GROUPS:
    _OFF[_n] = (_o, _o + _w)
    _o += _w
W_IN_PACKED = _o
W_IN_T = W_KA + SUBLANE
_DSA_KEY_CHUNK = 512


def _pad_heads(w, n_heads, hd):
    d = w.shape[0]
    w = w.reshape(d, n_heads, hd)
    return jnp.pad(w, ((0, 0), (0, 0), (0, _HP - hd))).reshape(d, n_heads * _HP)


def _pack_w_in(w_in):
    ends = np.cumsum(IN_SIZES)
    starts = ends - np.asarray(IN_SIZES)
    qa, ka, va, qi, ki, wi, qg, kg, vg, gk, go = [w_in[:, a:b] for a, b in zip(starts, ends)]
    d = w_in.shape[0]
    pad_to = lambda w: jnp.pad(w, ((0, 0), (0, _HP - w.shape[1])))
    cols = [_pad_heads(qa, N_ATTN_HEADS, ATTN_HEAD_DIM), _pad_heads(ka, N_KV_HEADS, ATTN_HEAD_DIM),
            _pad_heads(qi, N_IDX_HEADS, IDX_HEAD_DIM), _pad_heads(ki, 1, IDX_HEAD_DIM),
            qg, kg, vg, go, pad_to(gk)]
    w = jnp.concatenate(cols, axis=1).astype(BF16)
    assert w.shape == (d, W_IN_PACKED)
    w_t = jnp.concatenate([_pad_heads(va, N_KV_HEADS, ATTN_HEAD_DIM),
                           jnp.pad(wi, ((0, 0), (0, SUBLANE - N_IDX_HEADS)))], axis=1).T.astype(BF16)
    assert w_t.shape == (W_IN_T, d)
    return w, w_t


def _inproj_kernel(x_ref, sc_ref, sh_ref, n1_ref, w_ref, wt_ref, cos_ref, s1_ref, s2_ref, qn_ref, kn_ref,
                   qa_ref, ka_ref, vt_ref, qi_ref, ki_ref, wit_ref,
                   qg_ref, kg_ref, vg_ref, gk_ref, go_ref):
    x = x_ref[...]
    ms = jnp.mean(x * x, axis=-1, keepdims=True)
    h = x * lax.rsqrt(ms + NORM_EPS) * n1_ref[...]
    h = h * (1.0 + sc_ref[0]) + sh_ref[0]
    hb = h.astype(BF16)
    cos = cos_ref[...]
    s1 = s1_ref[...]
    s2 = s2_ref[...]

    def proj(name):
        a, b = _OFF[name]
        return jnp.dot(hb, w_ref[:, a:b], preferred_element_type=F32)

    def rope(v):
        return v * cos + pltpu.roll(v, LANE - 8, 1) * s1 + pltpu.roll(v, 8, 1) * s2

    def head_norm(v, w):
        ssq = jnp.sum(v * v, axis=-1, keepdims=True)
        return v * lax.rsqrt(ssq * (1.0 / ATTN_HEAD_DIM) + NORM_EPS) * w

    p = proj("qa")
    for hh in range(N_ATTN_HEADS):
        sl = slice(hh * _HP, (hh + 1) * _HP)
        qa_ref[:, sl] = (rope(head_norm(p[:, sl], qn_ref[...])) * (ATTN_HEAD_DIM ** -0.5)).astype(BF16)
    p = proj("ka")
    for hh in range(N_KV_HEADS):
        sl = slice(hh * _HP, (hh + 1) * _HP)
        ka_ref[:, sl] = rope(head_norm(p[:, sl], kn_ref[...])).astype(BF16)
    pt = lax.dot_general(wt_ref[...], hb, NT_DIMS, preferred_element_type=F32)
    vt_ref[0] = pt[:W_KA].astype(BF16)
    wit_ref[...] = pt[W_KA:] * ((N_IDX_HEADS * IDX_HEAD_DIM) ** -0.5)
    p = proj("qi")
    for hh in range(N_IDX_HEADS):
        sl = slice(hh * _HP, (hh + 1) * _HP)
        qi_ref[:, sl] = rope(p[:, sl]).astype(BF16)
    ki_ref[...] = rope(proj("ki")).astype(BF16)
    p = proj("gqk")
    qg_ref[...] = p[:, :W_GQ]
    kg_ref[...] = p[:, W_GQ:]
    vg_ref[...] = proj("gv")
    go_ref[...] = proj("go")
    gk_ref[...] = proj("gk")


def _inproj(x2, sc1, sh1, n1w, w_packed, w_t, cos, s1, s2, qnw, knw, seq):
    t, d = x2.shape
    tm = min(_DSA_KEY_CHUNK, seq)
    per_b = seq // tm
    row = lambda w: pl.BlockSpec((tm, w), lambda i: (i, 0))
    mod = pl.BlockSpec((1, 1, d), lambda i: (i // per_b, 0, 0))
    one = lambda w: pl.BlockSpec((1, w), lambda i: (0, 0))
    sds = jax.ShapeDtypeStruct
    outs = (("qa", row(W_QA), (t, W_QA), BF16), ("ka", row(W_KA), (t, W_KA), BF16),
            ("vt", pl.BlockSpec((1, W_KA, tm), lambda i: (i, 0, 0)), (t // tm, W_KA, tm), BF16),
            ("qi", row(W_QI), (t, W_QI), BF16), ("ki", row(_HP), (t, _HP), BF16),
            ("wit", pl.BlockSpec((SUBLANE, tm), lambda i: (0, i)), (SUBLANE, t), F32),
            ("qg", row(W_GQ), (t, W_GQ), F32), ("kg", row(W_GQ), (t, W_GQ), F32),
            ("vg", row(W_GV), (t, W_GV), F32), ("gk", row(_HP), (t, _HP), F32),
            ("go", row(W_GV), (t, W_GV), F32))
    return pl.pallas_call(
        _inproj_kernel,
        grid=(t // tm,),
        in_specs=[row(d), mod, mod, one(d),
                  pl.BlockSpec((d, W_IN_PACKED), lambda i: (0, 0)),
                  pl.BlockSpec((W_IN_T, d), lambda i: (0, 0)),
                  row(LANE), row(LANE), row(LANE), one(LANE), one(LANE)],
        out_specs=[o[1] for o in outs],
        out_shape=[sds(o[2], o[3]) for o in outs],
        compiler_params=_cparams("arbitrary"),
        name="inproj",
    )(x2, sc1, sh1, n1w, w_packed, w_t, cos, s1, s2, qnw, knw)


_BISECT_STEPS = 8
_DSA_QUERY_TILE = 512
_DSA_SUB_ROWS = 128
_BISECT_MAX_ROUNDS = 96


_FOLD_ROWS = SUBLANE


def _fold_keys(v, op):
    return op(v.reshape(v.shape[0] // _FOLD_ROWS, _FOLD_ROWS, v.shape[1]), axis=0)


def _dsa_kernel(qa_ref, ka_ref, vt_ref, qi_ref, ki_ref, wit_ref, nw_ref, o_ref, sc_ref, jstar_ref,
                *, n_sel, tq, ch, sb, s_len):
    q0 = pl.program_id(1) * tq
    n_chunks = (q0 + tq + ch - 1) // ch
    n_sub = (q0 + tq + sb - 1) // sb
    per_chunk = ch // sb
    wit = wit_ref[...]
    w_rows = [wit[hh:hh + 1, :] for hh in range(N_IDX_HEADS)]
    qcol = q0 + lax.broadcasted_iota(I32, (ch, tq), 1)
    krow = lax.broadcasted_iota(I32, (ch, tq), 0)
    krow_sb = lax.broadcasted_iota(I32, (sb, tq), 0)
    neg = jnp.full((_FOLD_ROWS, tq), -jnp.inf, F32)
    pos = jnp.full((_FOLD_ROWS, tq), jnp.inf, F32)
    zero = jnp.zeros((_FOLD_ROWS, tq), F32)
    col_sum = lambda v: jnp.sum(v, axis=0, keepdims=True)
    col_max = lambda v: jnp.max(v, axis=0, keepdims=True)
    col_min = lambda v: jnp.min(v, axis=0, keepdims=True)

    def score_chunk(c, carry):
        top, bot = carry
        k0 = pl.multiple_of(c * ch, ch)
        kc = ki_ref[pl.ds(k0, ch), :]
        sc = None
        for hh in range(N_IDX_HEADS):
            d = lax.dot_general(kc, qi_ref[:, hh * _HP:(hh + 1) * _HP], NT_DIMS,
                                preferred_element_type=F32)
            term = w_rows[hh] * jnp.maximum(d, 0.0)
            sc = term if sc is None else sc + term
        causal = (k0 + krow) <= qcol
        masked = jnp.where(causal, sc, -jnp.inf)
        for i in range(per_chunk):
            sc_ref[c * per_chunk + i] = masked[i * sb:(i + 1) * sb]
        top = jnp.maximum(top, _fold_keys(jnp.where(causal, sc, -jnp.inf), jnp.max))
        bot = jnp.minimum(bot, _fold_keys(jnp.where(causal, sc, jnp.inf), jnp.min))
        return top, bot

    top, bot = lax.fori_loop(0, n_chunks, score_chunk, (neg, pos))

    n_causal = (q0 + 1 + lax.broadcasted_iota(I32, (1, tq), 1)).astype(F32)
    kk = jnp.minimum(n_causal, float(n_sel))
    lo0 = col_min(bot)
    hi0 = col_max(top)
    hi0 = jnp.where(n_causal <= float(n_sel), lo0, hi0)

    def count_above(piv):
        def body(j, acc):
            return acc + _fold_keys(jnp.where(sc_ref[j] > piv, 1.0, 0.0), jnp.sum)
        return col_sum(lax.fori_loop(0, n_sub, body, zero))

    def search_round(carry):
        lo, hi, _, it = carry
        for _ in range(_BISECT_STEPS):
            mid = lo + 0.5 * (hi - lo)
            stuck = jnp.logical_or(mid <= lo, mid >= hi)
            piv = jnp.where(stuck, lo, mid)
            below = count_above(piv) < kk
            hi_n = jnp.where(below, piv, hi)
            lo = jnp.where(below, lo, jnp.where(stuck, hi, piv))
            hi = hi_n

        def tighten(j, carry):
            t_acc, b_acc = carry
            s = sc_ref[j]
            t_acc = jnp.maximum(t_acc, _fold_keys(jnp.where(s <= hi, s, -jnp.inf), jnp.max))
            b_acc = jnp.minimum(b_acc, _fold_keys(jnp.where(s >= lo, s, jnp.inf), jnp.min))
            return t_acc, b_acc

        t_acc, b_acc = lax.fori_loop(0, n_sub, tighten, (neg, pos))
        top_v = col_max(t_acc)
        bot_v = col_min(b_acc)
        open_cols = jnp.max(jnp.where(top_v > bot_v, 1, 0))
        return bot_v, top_v, open_cols, it + 1

    def search_open(carry):
        return jnp.logical_and(carry[2] > 0, carry[3] < _BISECT_MAX_ROUNDS)

    open0 = jnp.max(jnp.where(hi0 > lo0, 1, 0))
    _, theta, _, _ = lax.while_loop(search_open, search_round, (lo0, hi0, open0, jnp.int32(0)))

    def tally(j, carry):
        a_acc, e_acc = carry
        s = sc_ref[j]
        return (a_acc + _fold_keys(jnp.where(s > theta, 1.0, 0.0), jnp.sum),
                e_acc + _fold_keys(jnp.where(s == theta, 1.0, 0.0), jnp.sum))

    a_acc, e_acc = lax.fori_loop(0, n_sub, tally, (zero, zero))
    need = kk - col_sum(a_acc)
    n_tied = col_sum(e_acc)
    jstar_ref[...] = jnp.full(jstar_ref.shape, s_len, I32)

    @pl.when(jnp.max(jnp.where(n_tied > need, 1, 0)) > 0)
    def _():
        def step(_, bounds):
            lo_i, hi_i = bounds
            mid = (lo_i + hi_i) >> 1

            def body(j, acc):
                hit = jnp.where(sc_ref[j] == theta, jnp.where(j * sb + krow_sb <= mid, 1.0, 0.0), 0.0)
                return acc + _fold_keys(hit, jnp.sum)

            ok = col_sum(lax.fori_loop(0, n_sub, body, zero)) >= need
            return jnp.where(ok, lo_i, mid), jnp.where(ok, mid, hi_i)

        n_steps = int(np.ceil(np.log2(s_len))) + 1
        _, hi_i = lax.fori_loop(0, n_steps, step,
                                (jnp.full((1, tq), -1, I32), jnp.full((1, tq), s_len - 1, I32)))
        jstar_ref[...] = jnp.broadcast_to(hi_i, jstar_ref.shape)

    jstar = jstar_ref[0:1, :]

    def to_bias(j, carry):
        s = sc_ref[j]
        sc_ref[j] = jnp.where(s == theta, jnp.where(j * sb + krow_sb <= jstar, 0.0, -jnp.inf),
                              jnp.where(s > theta, 0.0, -jnp.inf))
        return carry

    lax.fori_loop(0, n_sub, to_bias, 0)

    rep = N_ATTN_HEADS // N_KV_HEADS
    qw = min(LANE, tq)
    kw = min(2 * LANE, ch)
    for g in range(N_KV_HEADS):
        gs = slice(g * _HP, (g + 1) * _HP)

        def att_chunk(c, carry):
            k0 = pl.multiple_of(c * ch, ch)
            kc = ka_ref[pl.ds(k0, ch), gs]
            vt = vt_ref[c, gs, :]
            bias = jnp.concatenate([sc_ref[c * per_chunk + i] for i in range(per_chunk)], axis=0)
            out = []
            for r in range(rep):
                m_all, den_all, acc_all = carry[r]
                hh = g * rep + r
                ms, dens, accs = [], [], []
                for qt in range(tq // qw):
                    ql = slice(qt * qw, (qt + 1) * qw)
                    m, den, acc = m_all[:, ql], den_all[:, ql], acc_all[:, ql]
                    q_h = qa_ref[ql, hh * _HP:(hh + 1) * _HP]
                    for kt in range(ch // kw):
                        kl = slice(kt * kw, (kt + 1) * kw)
                        s = lax.dot_general(kc[kl], q_h, NT_DIMS, preferred_element_type=F32) + bias[kl, ql]
                        m_new = jnp.maximum(m, col_max(_fold_keys(s, jnp.max)))
                        m_use = jnp.where(m_new == -jnp.inf, 0.0, m_new)
                        alpha = jnp.exp(m - m_use)
                        p = jnp.exp(s - m_use)
                        den = alpha * den + col_sum(_fold_keys(p, jnp.sum))
                        acc = alpha * acc + jnp.dot(vt[:, kl], p.astype(BF16), preferred_element_type=F32)
                        m = m_new
                    ms.append(m)
                    dens.append(den)
                    accs.append(acc)
                out.append(tuple(jnp.concatenate(v, axis=1) for v in (ms, dens, accs)))
            return tuple(out)

        init = tuple((jnp.full((1, tq), -jnp.inf, F32), jnp.zeros((1, tq), F32),
                      jnp.zeros((_HP, tq), F32)) for _ in range(rep))
        res = lax.fori_loop(0, n_chunks, att_chunk, init)
        for r in range(rep):
            hh = g * rep + r
            _, den, acc = res[r]
            o = acc / den
            ssq = col_sum(o * o)
            on = (o * lax.rsqrt(ssq * (1.0 / ATTN_HEAD_DIM) + NORM_EPS)).T * nw_ref[hh:hh + 1, :]
            o_ref[:, hh * _HP:(hh + 1) * _HP] = on.astype(BF16)


def _dsa(qa, ka, vt, qi, ki, wit, nw, bsz, seq):
    t = qa.shape[0]
    tq = min(_DSA_QUERY_TILE, seq)
    ch = min(_DSA_KEY_CHUNK, seq)
    sb = min(_DSA_SUB_ROWS, ch)
    nq = seq // tq
    nc = seq // ch
    n_sel = min(TOPK_KEYS, seq // 4)
    qrow = lambda w: pl.BlockSpec((tq, w), lambda b, j: (b * nq + j, 0))
    krow = lambda w: pl.BlockSpec((seq, w), lambda b, j: (b, 0))
    return pl.pallas_call(
        functools.partial(_dsa_kernel, n_sel=n_sel, tq=tq, ch=ch, sb=sb, s_len=seq),
        grid=(bsz, nq),
        in_specs=[qrow(W_QA), krow(W_KA), pl.BlockSpec((nc, W_KA, ch), lambda b, j: (b, 0, 0)),
                  qrow(W_QI), krow(_HP), pl.BlockSpec((SUBLANE, tq), lambda b, j: (0, b * nq + j)),
                  pl.BlockSpec((N_ATTN_HEADS, _HP), lambda b, j: (0, 0))],
        out_specs=qrow(W_QA),
        out_shape=jax.ShapeDtypeStruct((t, W_QA), BF16),
        scratch_shapes=[pltpu.VMEM((seq // sb, sb, tq), F32), pltpu.VMEM((SUBLANE, tq), I32)],
        compiler_params=_cparams("arbitrary", "arbitrary"),
        name="dsa",
    )(qa, ka, vt, qi, ki, wit, nw)


_GLA_LEVELS = 6
_GLA_ROWS = 512


def _gla_consts():
    c = GLA_CHUNK
    hk = N_GLA_HEADS * GLA_KEY_DIM
    i = np.arange(c)
    cum = [np.tril(np.ones((c, c)))]
    rq, pm = [], []
    t = i[:, None]
    s = (np.arange(hk) % GLA_KEY_DIM)[None, :]
    pm.append((t == s).astype(np.float32))
    for lvl in range(1, _GLA_LEVELS + 1):
        blk, half = 1 << lvl, 1 << (lvl - 1)
        split = (i // blk) * blk + half - 1
        cum.append((np.arange(c)[None, :] <= split[:, None]).astype(np.float64))
        right = ((i // half) % 2 == 1)
        rq.append(np.broadcast_to(right[:, None], (c, hk)).astype(np.float32))
        pm.append(((t // blk == s // blk) & ((t // half) % 2 == 1) & ((s // half) % 2 == 0))
                  .astype(np.float32))
    r = np.arange(hk)
    bd_k = (r[:, None] // GLA_KEY_DIM == r[None, :] // GLA_KEY_DIM).astype(np.float32)
    cv = np.arange(N_GLA_HEADS * GLA_VAL_DIM)
    bd_v = (r[:, None] // GLA_KEY_DIM == cv[None, :] // GLA_VAL_DIM).astype(np.float32)
    return (jnp.asarray(np.concatenate(cum, 0), BF16), jnp.asarray(np.stack(rq)),
            jnp.asarray(np.stack(pm)), jnp.asarray(bd_k), jnp.asarray(bd_v),
            jnp.asarray(bd_v.T.copy()))


def _gla_kernel(qg_ref, kg_ref, vg_ref, gk_ref, go_ref, wgk_ref, bgk_ref, gnw_ref,
                cum_ref, rq_ref, pm_ref, bdk_ref, bdv_ref, bdvt_ref, o_ref, st_ref):
    c = GLA_CHUNK

    @pl.when(pl.program_id(1) == 0)
    def _():
        st_ref[...] = jnp.zeros(st_ref.shape, F32)

    bdk = bdk_ref[...]
    cum = cum_ref[...]
    for ci in range(_GLA_ROWS // c):
        sl = slice(ci * c, (ci + 1) * c)
        q = qg_ref[sl, :] * (GLA_KEY_DIM ** -0.5)
        k = kg_ref[sl, :]
        v = vg_ref[sl, :]
        z = jnp.dot(gk_ref[sl, :].astype(BF16), wgk_ref[...], preferred_element_type=F32) + bgk_ref[...]
        la = (jnp.minimum(z, 0.0) - jnp.log1p(jnp.exp(-jnp.abs(z)))) * (1.0 / GLA_GATE_TAU)
        la_hi = la.astype(BF16)
        la_lo = (la - la_hi.astype(F32)).astype(BF16)
        cb = (jnp.dot(cum, la_hi, preferred_element_type=F32)
              + jnp.dot(cum, la_lo, preferred_element_type=F32))
        b = cb[0:c]
        kbd = (jnp.concatenate([k] * N_GLA_HEADS, axis=0) * bdk).astype(BF16)
        a = lax.dot_general(q.astype(BF16), kbd, NT_DIMS, preferred_element_type=F32) * pm_ref[0]
        for lvl in range(1, _GLA_LEVELS + 1):
            beta = cb[lvl * c:(lvl + 1) * c]
            rq = rq_ref[lvl - 1]
            qs = q * jnp.exp(jnp.minimum(b - beta, 0.0)) * rq
            ks = k * jnp.exp(jnp.minimum(beta - b, 0.0)) * (1.0 - rq)
            ksbd = (jnp.concatenate([ks] * N_GLA_HEADS, axis=0) * bdk).astype(BF16)
            a = a + lax.dot_general(qs.astype(BF16), ksbd, NT_DIMS,
                                    preferred_element_type=F32) * pm_ref[lvl]
        vbd = (jnp.concatenate([v] * N_GLA_HEADS, axis=0) * bdv_ref[...]).astype(BF16)
        intra = jnp.dot(a.astype(BF16), vbd, preferred_element_type=F32)
        st = st_ref[...]
        inter = lax.dot_general((q * jnp.exp(b)).astype(BF16), st.astype(BF16), NT_DIMS,
                                preferred_element_type=F32)
        o = intra + inter
        b_end = b[c - 1:c, :]
        k_dec = (k * jnp.exp(b_end - b)).astype(BF16)
        upd = lax.dot_general(v.astype(BF16), k_dec, TN_DIMS, preferred_element_type=F32)
        st_ref[...] = st * jnp.exp(b_end) + upd * bdvt_ref[...]
        for hh in range(N_GLA_HEADS):
            hs = slice(hh * GLA_VAL_DIM, (hh + 1) * GLA_VAL_DIM)
            oh = o[:, hs]
            ms = jnp.mean(oh * oh, axis=-1, keepdims=True)
            on = oh * lax.rsqrt(ms + NORM_EPS) * gnw_ref[:, hs]
            o_ref[sl, hs] = (on * _silu(go_ref[sl, hs])).astype(BF16)


def _gla(qg, kg, vg, gk, go, wgk, bgk, gnw, bsz, seq):
    t = qg.shape[0]
    rows = _GLA_ROWS
    assert seq % rows == 0
    nj = seq // rows
    consts = _gla_consts()
    row = lambda w: pl.BlockSpec((rows, w), lambda b, j: (b * nj + j, 0))
    full = lambda a: pl.BlockSpec(a.shape, lambda b, j: (0,) * a.ndim)
    args = (qg, kg, vg, gk, go, wgk, bgk, gnw) + consts
    return pl.pallas_call(
        _gla_kernel,
        grid=(bsz, nj),
        in_specs=[row(W_GQ), row(W_GQ), row(W_GV), row(_HP), row(W_GV)] + [full(a) for a in args[5:]],
        out_specs=row(W_GV),
        out_shape=jax.ShapeDtypeStruct((t, W_GV), BF16),
        scratch_shapes=[pltpu.VMEM((W_GV, W_GQ), F32)],
        compiler_params=_cparams("arbitrary", "arbitrary"),
        name="gla",
    )(*args)


def _outproj_kernel(attn_ref, gla_ref, x_ref, g1_ref, sc_ref, sh_ref, n2_ref, wa_ref, wb_ref,
                    wr_ref, br_ref, tri_ref,
                    x1_ref, h2_ref, idx_ref, wsel_ref, rank_ref, ccol_ref, crow_ref, *, tm):
    @pl.when(pl.program_id(0) == 0)
    def _():
        ccol_ref[...] = jnp.zeros(ccol_ref.shape, F32)
        crow_ref[...] = jnp.zeros(crow_ref.shape, F32)

    mixed = (jnp.dot(attn_ref[...], wa_ref[...], preferred_element_type=F32)
             + jnp.dot(gla_ref[...], wb_ref[...], preferred_element_type=F32))
    x1 = x_ref[...] + g1_ref[0] * mixed
    x1_ref[...] = x1
    ms = jnp.mean(x1 * x1, axis=-1, keepdims=True)
    h2 = x1 * lax.rsqrt(ms + NORM_EPS) * n2_ref[...]
    h2 = h2 * (1.0 + sc_ref[0]) + sh_ref[0]
    _store_slabs(h2_ref, 0, tm, _pack_row_halves(h2))

    logits = lax.dot_general(wr_ref[...], h2.astype(BF16), NT_DIMS, preferred_element_type=F32)
    scores = jax.nn.sigmoid(logits)
    biased = scores + jnp.concatenate([br_ref[...]] * (tm // LANE), axis=1)
    per_group = N_EXPERTS // N_EXPERT_GROUPS
    gs = []
    for g in range(N_EXPERT_GROUPS):
        blk = biased[g * per_group:(g + 1) * per_group]
        m1 = jnp.max(blk, axis=0, keepdims=True)
        n1 = jnp.sum(jnp.where(blk == m1, 1.0, 0.0), axis=0, keepdims=True)
        m2 = jnp.max(jnp.where(blk < m1, blk, -jnp.inf), axis=0, keepdims=True)
        gs.append(m1 + jnp.where(n1 >= 2.0, m1, m2))
    gscore = jnp.concatenate(gs, axis=0)
    gi = lax.broadcasted_iota(I32, gscore.shape, 0)
    gsel = jnp.zeros(gscore.shape, F32)
    for _ in range(TOPK_GROUPS):
        m = jnp.max(gscore, axis=0, keepdims=True)
        first = jnp.min(jnp.where(gscore == m, gi, N_EXPERT_GROUPS), axis=0, keepdims=True)
        hit = gi == first
        gsel = jnp.where(hit, 1.0, gsel)
        gscore = jnp.where(hit, -jnp.inf, gscore)
    gmask = jnp.concatenate(
        [jnp.broadcast_to(gsel[g:g + 1], (per_group, tm)) for g in range(N_EXPERT_GROUPS)], axis=0)
    masked = jnp.where(gmask > 0.0, biased, -jnp.inf)
    ei = lax.broadcasted_iota(I32, masked.shape, 0)
    chosen = jnp.zeros(masked.shape, F32)
    idxs, ws = [], []
    for _ in range(TOPK_EXPERTS):
        m = jnp.max(masked, axis=0, keepdims=True)
        first = jnp.min(jnp.where(masked == m, ei, N_EXPERTS), axis=0, keepdims=True)
        hit = ei == first
        idxs.append(first)
        ws.append(jnp.sum(jnp.where(hit, scores, 0.0), axis=0, keepdims=True))
        chosen = jnp.where(hit, 1.0, chosen)
        masked = jnp.where(hit, -jnp.inf, masked)
    wsum = ws[0]
    for w in ws[1:]:
        wsum = wsum + w
    idx_ref[...] = jnp.concatenate(idxs, axis=0)
    wsel_ref[...] = jnp.concatenate([w / wsum * ROUTED_SCALE for w in ws], axis=0)

    chosen_b = chosen.astype(BF16)
    before = jnp.dot(chosen_b, tri_ref[...], preferred_element_type=F32)
    seen = ccol_ref[...]
    pos = before + jnp.concatenate([seen] * (tm // LANE), axis=1)
    rank_ref[...] = jnp.concatenate(
        [jnp.sum(jnp.where(ei == i, pos, 0.0), axis=0, keepdims=True) for i in idxs],
        axis=0).astype(I32)
    ccol_ref[...] = seen + jnp.sum(chosen, axis=1, keepdims=True)
    crow_ref[...] = crow_ref[...] + lax.dot_general(
        jnp.ones((SUBLANE, tm), BF16), chosen_b, NT_DIMS, preferred_element_type=F32)


def _outproj(attn, gla, x2, g1, sc2, sh2, n2w, wa, wb, wr_t, br, seq):
    t, d = x2.shape
    tm = min(512, seq)
    per_b = seq // tm
    row = lambda w: pl.BlockSpec((tm, w), lambda i: (i, 0))
    col = lambda: pl.BlockSpec((TOPK_EXPERTS, tm), lambda i: (0, i))
    mod = pl.BlockSpec((1, 1, d), lambda i: (i // per_b, 0, 0))
    full = lambda a: pl.BlockSpec(a.shape, lambda i: (0,) * a.ndim)
    tri = jnp.asarray(np.triu(np.ones((tm, tm), np.float32), 1), BF16)
    br_col = jnp.broadcast_to(br.reshape(N_EXPERTS, 1).astype(F32), (N_EXPERTS, LANE))
    n2 = n2w.reshape(1, d)
    n_slabs = d // 2 // LANE
    sel = jax.ShapeDtypeStruct((TOPK_EXPERTS, t), I32)
    return pl.pallas_call(
        functools.partial(_outproj_kernel, tm=tm),
        grid=(t // tm,),
        in_specs=[row(W_QA), row(W_GV), row(d), mod, mod, mod, full(n2), full(wa), full(wb),
                  full(wr_t), full(br_col), full(tri)],
        out_specs=[row(d), pl.BlockSpec((tm * n_slabs, LANE), lambda i: (i, 0)), col(), col(), col(),
                   pl.BlockSpec((N_EXPERTS, LANE), lambda i: (0, 0)),
                   pl.BlockSpec((SUBLANE, N_EXPERTS), lambda i: (0, 0))],
        out_shape=[jax.ShapeDtypeStruct((t, d), F32), jax.ShapeDtypeStruct((t * n_slabs, LANE), U32),
                   sel, jax.ShapeDtypeStruct((TOPK_EXPERTS, t), F32), sel,
                   jax.ShapeDtypeStruct((N_EXPERTS, LANE), F32),
                   jax.ShapeDtypeStruct((SUBLANE, N_EXPERTS), F32)],
        compiler_params=_cparams("arbitrary"),
        name="outproj",
    )(attn, gla, x2, g1, sc2, sh2, n2, wa, wb, wr_t, br_col, tri)


def _plan_kernel(ccol_ref, crow_ref, idx_ref, rank_ref, dest_ref, blk_ref, nblk_ref, *, n_blocks):
    m = float(EXPERT_ROWS)
    pad_row = jnp.floor((crow_ref[0:1, :] + (m - 1.0)) * (1.0 / m)) * m
    pad_col = jnp.floor((ccol_ref[:, 0:1] + (m - 1.0)) * (1.0 / m)) * m
    er = lax.broadcasted_iota(I32, (N_EXPERTS, N_EXPERTS), 0)
    ec = lax.broadcasted_iota(I32, (N_EXPERTS, N_EXPERTS), 1)
    start_col = jnp.sum(jnp.where(ec < er, pad_row, 0.0), axis=1, keepdims=True)
    end_col = start_col + pad_col
    first_row = (lax.broadcasted_iota(I32, (N_EXPERTS, n_blocks), 1) * EXPERT_ROWS).astype(F32)
    owner = jnp.sum(jnp.where(end_col <= first_row, 1.0, 0.0), axis=0, keepdims=True)
    blk_ref[...] = jnp.minimum(owner, float(N_EXPERTS - 1)).astype(I32)
    total = jnp.sum(pad_row, axis=1, keepdims=True)
    nblk_ref[...] = jnp.broadcast_to(total * (1.0 / m), nblk_ref.shape).astype(I32)
    tm = idx_ref.shape[1]
    ei = lax.broadcasted_iota(I32, (N_EXPERTS, tm), 0)
    rows = []
    for kc in range(TOPK_EXPERTS):
        hit = ei == idx_ref[kc:kc + 1, :]
        rows.append(jnp.sum(jnp.where(hit, start_col, 0.0), axis=0, keepdims=True))
    dest_ref[...] = jnp.concatenate(rows, axis=0).astype(I32) + rank_ref[...]


def _plan(ccol, crow, idx, rank, n_blocks):
    t = idx.shape[1]
    tm = min(1024, t)
    col = pl.BlockSpec((TOPK_EXPERTS, tm), lambda i: (0, i))
    full = lambda a: pl.BlockSpec(a.shape, lambda i: (0,) * a.ndim)
    return pl.pallas_call(
        functools.partial(_plan_kernel, n_blocks=n_blocks),
        grid=(t // tm,),
        in_specs=[full(ccol), full(crow), col, col],
        out_specs=[col, pl.BlockSpec((1, n_blocks), lambda i: (0, 0)),
                   pl.BlockSpec((1, LANE), lambda i: (0, 0))],
        out_shape=[jax.ShapeDtypeStruct((TOPK_EXPERTS, t), I32),
                   jax.ShapeDtypeStruct((1, n_blocks), I32),
                   jax.ShapeDtypeStruct((1, LANE), I32)],
        compiler_params=_cparams("arbitrary"),
        name="plan",
    )(ccol, crow, idx, rank)


def _row_copy(src, src_row, dst, dst_row, sem, n_slabs):
    return pltpu.make_async_copy(src.at[pl.ds(src_row * n_slabs, n_slabs), :],
                                 dst.at[pl.ds(dst_row * n_slabs, n_slabs), :], sem)


def _scatter_kernel(dest_ref, h_ref, xb_in_ref, xb_ref, sem, *, tm, n_slabs):
    del xb_in_ref

    def issue(tok, carry):
        for kc in range(TOPK_EXPERTS):
            _row_copy(h_ref, tok, xb_ref, dest_ref[kc, tok], sem, n_slabs).start()
        return carry

    lax.fori_loop(0, tm, issue, 0)

    def drain(tok, carry):
        for kc in range(TOPK_EXPERTS):
            _row_copy(h_ref, tok, xb_ref, dest_ref[kc, tok], sem, n_slabs).wait()
        return carry

    lax.fori_loop(0, tm, drain, 0)


def _scatter(dest, h_rows, xb_zero, n_slabs):
    t = dest.shape[1]
    tm = min(512, t)
    return pl.pallas_call(
        functools.partial(_scatter_kernel, tm=tm, n_slabs=n_slabs),
        grid=(t // tm,),
        in_specs=[pl.BlockSpec((TOPK_EXPERTS, tm), lambda i: (0, i), memory_space=pltpu.SMEM),
                  pl.BlockSpec((tm * n_slabs, LANE), lambda i: (i, 0)),
                  pl.BlockSpec(memory_space=pl.ANY)],
        out_specs=pl.BlockSpec(memory_space=pl.ANY),
        out_shape=jax.ShapeDtypeStruct(xb_zero.shape, U32),
        scratch_shapes=[pltpu.SemaphoreType.DMA],
        input_output_aliases={2: 0},
        compiler_params=_cparams("arbitrary"),
        name="scatter",
    )(dest, h_rows, xb_zero)


def _experts_kernel(blk_ref, nblk_ref, x_ref, wg_ref, wu_ref, wd_ref, y_ref, wgb, wub, wdb, *, n_slabs):
    i = pl.program_id(0)
    m = EXPERT_ROWS

    @pl.when(i >= nblk_ref[0])
    def _():
        y_ref[...] = jnp.zeros(y_ref.shape, U32)

    @pl.when(i < nblk_ref[0])
    def _():
        prev = blk_ref[jnp.maximum(i - 1, 0)]

        @pl.when(jnp.logical_or(i == 0, blk_ref[i] != prev))
        def _():
            wgb[...] = wg_ref[0].astype(BF16)
            wub[...] = wu_ref[0].astype(BF16)
            wdb[...] = wd_ref[0].astype(BF16)

        x = _unpack_row_halves(_load_slabs(x_ref, 0, m, n_slabs)).astype(BF16)
        g = jnp.dot(x, wgb[...], preferred_element_type=F32)
        u = jnp.dot(x, wub[...], preferred_element_type=F32)
        y = jnp.dot((_silu(g) * u).astype(BF16), wdb[...], preferred_element_type=F32)
        _store_slabs(y_ref, 0, m, _pack_row_halves(y))


def _experts(blk, nblk, xb, we_gate, we_up, we_down, n_blocks):
    m = EXPERT_ROWS
    e, d, f = we_gate.shape
    n_slabs = d // 2 // LANE
    live = lambda i, blk, nblk: jnp.minimum(i, jnp.maximum(nblk[0] - 1, 0))
    grid_spec = pltpu.PrefetchScalarGridSpec(
        num_scalar_prefetch=2,
        grid=(n_blocks,),
        in_specs=[pl.BlockSpec((m * n_slabs, LANE), lambda i, blk, nblk: (live(i, blk, nblk), 0)),
                  pl.BlockSpec((1, d, f), lambda i, blk, nblk: (blk[live(i, blk, nblk)], 0, 0)),
                  pl.BlockSpec((1, d, f), lambda i, blk, nblk: (blk[live(i, blk, nblk)], 0, 0)),
                  pl.BlockSpec((1, f, d), lambda i, blk, nblk: (blk[live(i, blk, nblk)], 0, 0))],
        out_specs=pl.BlockSpec((m * n_slabs, LANE), lambda i, blk, nblk: (i, 0)),
        scratch_shapes=[pltpu.VMEM((d, f), BF16), pltpu.VMEM((d, f), BF16), pltpu.VMEM((f, d), BF16)],
    )
    return pl.pallas_call(
        functools.partial(_experts_kernel, n_slabs=n_slabs),
        grid_spec=grid_spec,
        out_shape=jax.ShapeDtypeStruct(xb.shape, U32),
        compiler_params=_cparams("arbitrary"),
        name="experts",
    )(blk, nblk, xb, we_gate, we_up, we_down)


def _combine_kernel(dest_ref, yb_ref, w_ref, h_ref, x1_ref, g2_ref, wsg_ref, wsu_ref, wsd_ref,
                    o_ref, buf, sem, *, tm, n_slabs):
    def issue(tok, carry):
        for kc in range(TOPK_EXPERTS):
            _row_copy(yb_ref, dest_ref[kc, tok], buf, kc * tm + tok, sem, n_slabs).start()
        return carry

    lax.fori_loop(0, tm, issue, 0)

    hb = _unpack_row_halves(_load_slabs(h_ref, 0, tm, n_slabs)).astype(BF16)
    g = jnp.dot(hb, wsg_ref[...], preferred_element_type=F32)
    u = jnp.dot(hb, wsu_ref[...], preferred_element_type=F32)
    shared = jnp.dot((_silu(g) * u).astype(BF16), wsd_ref[...], preferred_element_type=F32)

    def drain(tok, carry):
        for kc in range(TOPK_EXPERTS):
            _row_copy(yb_ref, dest_ref[kc, tok], buf, kc * tm + tok, sem, n_slabs).wait()
        return carry

    lax.fori_loop(0, tm, drain, 0)

    w = w_ref[...]
    routed = None
    for kc in range(TOPK_EXPERTS):
        term = w[:, kc:kc + 1] * _unpack_row_halves(_load_slabs(buf, kc * tm, tm, n_slabs))
        routed = term if routed is None else routed + term
    o_ref[...] = x1_ref[...] + g2_ref[0] * (routed + shared)


def _combine(dest, yb, w_tok, h_rows, x1, g2, wsg, wsu, wsd, seq):
    t, d = x1.shape
    tm = min(256, seq)
    per_b = seq // tm
    n_slabs = d // 2 // LANE
    row = lambda w: pl.BlockSpec((tm, w), lambda i: (i, 0))
    full = lambda a: pl.BlockSpec(a.shape, lambda i: (0,) * a.ndim)
    return pl.pallas_call(
        functools.partial(_combine_kernel, tm=tm, n_slabs=n_slabs),
        grid=(t // tm,),
        in_specs=[pl.BlockSpec((TOPK_EXPERTS, tm), lambda i: (0, i), memory_space=pltpu.SMEM),
                  pl.BlockSpec(memory_space=pl.ANY),
                  row(TOPK_EXPERTS), pl.BlockSpec((tm * n_slabs, LANE), lambda i: (i, 0)), row(d),
                  pl.BlockSpec((1, 1, d), lambda i: (i // per_b, 0, 0)),
                  full(wsg), full(wsu), full(wsd)],
        out_specs=row(d),
        out_shape=jax.ShapeDtypeStruct((t, d), F32),
        scratch_shapes=[pltpu.VMEM((TOPK_EXPERTS * tm * n_slabs, LANE), U32),
                        pltpu.SemaphoreType.DMA],
        compiler_params=_cparams("arbitrary"),
        name="combine",
    )(dest, yb, w_tok, h_rows, x1, g2, wsg, wsu, wsd)


def _layer(x2, cond_mod, positions_tables, p, bsz, seq):
    t, d = x2.shape
    cos, s1, s2 = positions_tables
    sh1, sc1, g1, sh2, sc2, g2 = [m.reshape(bsz, 1, d) for m in jnp.split(cond_mod, 6, axis=-1)]

    lane_pad = lambda w: jnp.pad(w.astype(F32), ((0, 0), (0, _HP - w.shape[-1])))
    qnw = lane_pad(p["q_norm_w"].reshape(1, ATTN_HEAD_DIM))
    knw = lane_pad(p["k_norm_w"].reshape(1, ATTN_HEAD_DIM))
    qa, ka, vt, qi, ki, wit, qg, kg, vg, gk, go = _inproj(
        x2, sc1, sh1, p["norm1_w"].reshape(1, d), *_pack_w_in(p["w_in"]), cos, s1, s2, qnw, knw, seq)

    attn = _dsa(qa, ka, vt, qi, ki, wit, lane_pad(p["attn_out_norm_w"]), bsz, seq)

    wgk = jnp.pad(p["w_gk2"], ((0, _HP - GLA_GATE_RANK), (0, 0))).astype(BF16)
    gla = _gla(qg, kg, vg, gk, go, wgk, p["b_gk"].reshape(1, W_GQ).astype(F32),
               p["gla_norm_w"].reshape(1, W_GV).astype(F32), bsz, seq)

    w_out = p["w_out"]
    n_attn = N_ATTN_HEADS * ATTN_HEAD_DIM
    wa = jnp.pad(w_out[:n_attn].reshape(N_ATTN_HEADS, ATTN_HEAD_DIM, d),
                 ((0, 0), (0, _HP - ATTN_HEAD_DIM), (0, 0))).reshape(W_QA, d).astype(BF16)
    wb = w_out[n_attn:].astype(BF16)
    x1, h2, idx, wsel, rank, ccol, crow = _outproj(
        attn, gla, x2, g1, sc2, sh2, p["norm2_w"], wa, wb,
        p["w_router"].T.astype(BF16), p["b_router"], seq)

    n_blocks = (t * TOPK_EXPERTS) // EXPERT_ROWS + N_EXPERTS
    dest, blk, nblk = _plan(ccol, crow, idx, rank, n_blocks)

    n_rows = n_blocks * EXPERT_ROWS
    n_slabs = d // 2 // LANE
    xb = _scatter(dest, h2, jnp.zeros((n_rows * n_slabs, LANE), U32), n_slabs)
    yb = _experts(blk.reshape(n_blocks), nblk.reshape(LANE)[:1], xb,
                  p["we_gate"], p["we_up"], p["we_down"], n_blocks)
    return _combine(dest, yb, wsel.T, h2, x1, g2, p["ws_gate"].astype(BF16),
                    p["ws_up"].astype(BF16), p["ws_down"].astype(BF16), seq)


def kernel(x, c, positions, norm1_w, norm2_w, w_ada, b_ada, w_in, q_norm_w, k_norm_w,
           attn_out_norm_w, w_gk2, b_gk, gla_norm_w, w_out, w_router, b_router,
           we_gate, we_up, we_down, ws_gate, ws_up, ws_down):
    bsz, seq, d = x.shape
    depth = w_ada.shape[0]
    x2 = x.reshape(bsz * seq, d)
    tables = _rope_tables(positions)
    stacked = dict(norm1_w=norm1_w, norm2_w=norm2_w, w_in=w_in, q_norm_w=q_norm_w, k_norm_w=k_norm_w,
                   attn_out_norm_w=attn_out_norm_w, w_gk2=w_gk2, b_gk=b_gk, gla_norm_w=gla_norm_w,
                   w_out=w_out, w_router=w_router, b_router=b_router, we_gate=we_gate, we_up=we_up,
                   we_down=we_down, ws_gate=ws_gate, ws_up=ws_up, ws_down=ws_down)
    for l in range(depth):
        mod = _ada(c.astype(F32), w_ada[l], b_ada[l])
        x2 = _layer(x2, mod, tables, {k: v[l] for k, v in stacked.items()}, bsz, seq)
    return x2.reshape(bsz, seq, d)
```

```python
import functools

import numpy as np
import jax
import jax.numpy as jnp
from jax import lax
from jax.experimental import pallas as pl
from jax.experimental.pallas import tpu as pltpu

F32 = jnp.float32
BF16 = jnp.bfloat16
I32 = jnp.int32

LANE = 128
SUBLANE = 8
VMEM_LIMIT = 48 * 1024 * 1024

N_ATTN_HEADS = 8
ATTN_HEAD_DIM = 64
N_KV_HEADS = 2
N_IDX_HEADS = 4
IDX_HEAD_DIM = 64
TOPK_KEYS = 256
N_GLA_HEADS = 4
GLA_KEY_DIM = 64
GLA_VAL_DIM = 128
GLA_GATE_RANK = 16
GLA_GATE_TAU = 16.0
GLA_CHUNK = 64
ROPE_THETA = 500000.0
ROPE_FRACTION = 4
N_EXPERTS = 256
N_EXPERT_GROUPS = 8
TOPK_GROUPS = 4
TOPK_EXPERTS = 8
EXPERT_DIM = 256
ROUTED_SCALE = 2.5
NORM_EPS = 1e-6
IN_SIZES = (512, 128, 128, 256, 64, 4, 256, 256, 512, 16, 512)

U32 = jnp.uint32
EXPERT_ROWS = 512
NT_DIMS = (((1,), (1,)), ((), ()))
TN_DIMS = (((0,), (0,)), ((), ()))


def _cparams(*sem):
    return pltpu.CompilerParams(dimension_semantics=sem, vmem_limit_bytes=VMEM_LIMIT)


def _silu(v):
    return v * jax.nn.sigmoid(v)


def _pack_row_halves(v):
    half = v.shape[1] // 2
    bits = lambda a: lax.bitcast_convert_type(a.astype(jnp.bfloat16).astype(F32), U32)
    return bits(v[:, half:]) | (bits(v[:, :half]) >> 16)


def _unpack_row_halves(w):
    lo = lax.bitcast_convert_type(w << 16, F32)
    hi = lax.bitcast_convert_type(w & jnp.uint32(0xFFFF0000), F32)
    return jnp.concatenate([lo, hi], axis=1)


def _store_slabs(ref, row0, n_rows, w):
    n_slabs = w.shape[1] // LANE
    for sb in range(n_slabs):
        ref[pl.ds(row0 * n_slabs + sb, n_rows, stride=n_slabs), :] = w[:, sb * LANE:(sb + 1) * LANE]


def _load_slabs(ref, row0, n_rows, n_slabs):
    return jnp.concatenate(
        [ref[pl.ds(row0 * n_slabs + sb, n_rows, stride=n_slabs), :] for sb in range(n_slabs)], axis=1)


def _ada_kernel(c_ref, w_ref, b_ref, o_ref):
    cond = _silu(c_ref[...])
    o_ref[...] = jnp.dot(cond.astype(BF16), w_ref[...].astype(BF16),
                         preferred_element_type=F32) + b_ref[...]


def _ada(c, w, b):
    bsz, d = c.shape
    n = w.shape[1]
    tn = 1536
    return pl.pallas_call(
        _ada_kernel,
        grid=(n // tn,),
        in_specs=[pl.BlockSpec((bsz, d), lambda i: (0, 0)),
                  pl.BlockSpec((d, tn), lambda i: (0, i)),
                  pl.BlockSpec((1, tn), lambda i: (0, i))],
        out_specs=pl.BlockSpec((bsz, tn), lambda i: (0, i)),
        out_shape=jax.ShapeDtypeStruct((bsz, n), F32),
        compiler_params=_cparams("arbitrary"),
        name="ada",
    )(c, w, b.reshape(1, n))


def _rope_kernel(pos_ref, invf_ref, m1_ref, m2_ref, cos_ref, s1_ref, s2_ref):
    ang = pos_ref[...] * invf_ref[...]
    s = jnp.sin(ang)
    cos_ref[...] = jnp.cos(ang)
    s1_ref[...] = -s * m1_ref[...]
    s2_ref[...] = s * m2_ref[...]


def _rope_tables(positions):
    t = positions.size
    rot = ATTN_HEAD_DIM // ROPE_FRACTION
    half = rot // 2
    inv_freq = jnp.power(jnp.float32(ROPE_THETA), -jnp.arange(half, dtype=F32) / half)
    lane = np.arange(LANE) % ATTN_HEAD_DIM
    invf = jnp.where(lane < rot, inv_freq[lane % half], 0.0).astype(F32).reshape(1, LANE)
    m1 = jnp.asarray((lane < half).astype(np.float32)).reshape(1, LANE)
    m2 = jnp.asarray(((lane >= half) & (lane < rot)).astype(np.float32)).reshape(1, LANE)
    pos = jnp.broadcast_to(positions.reshape(t, 1).astype(F32), (t, LANE))
    tm = min(2048, t)
    row = pl.BlockSpec((tm, LANE), lambda i: (i, 0))
    one = pl.BlockSpec((1, LANE), lambda i: (0, 0))
    sds = jax.ShapeDtypeStruct((t, LANE), F32)
    return pl.pallas_call(
        _rope_kernel,
        grid=(t // tm,),
        in_specs=[row, one, one, one],
        out_specs=[row, row, row],
        out_shape=[sds, sds, sds],
        compiler_params=_cparams("arbitrary"),
        name="rope",
    )(pos, invf, m1, m2)


_HP = LANE
W_QA = N_ATTN_HEADS * _HP
W_KA = N_KV_HEADS * _HP
W_QI = N_IDX_HEADS * _HP
W_GQ = N_GLA_HEADS * GLA_KEY_DIM
W_GV = N_GLA_HEADS * GLA_VAL_DIM
_GROUPS = (("qa", W_QA), ("ka", W_KA), ("qi", W_QI), ("ki", _HP),
           ("gqk", 2 * W_GQ), ("gv", W_GV), ("go", W_GV), ("gk", _HP))
_OFF = {}
_o = 0
for _n, _w in _GROUPS:
    _OFF[_n] = (_o, _o + _w)
    _o += _w
W_IN_PACKED = _o
W_IN_T = W_KA + SUBLANE
_DSA_KEY_CHUNK = 512


def _pad_heads(w, n_heads, hd):
    d = w.shape[0]
    w = w.reshape(d, n_heads, hd)
    return jnp.pad(w, ((0, 0), (0, 0), (0, _HP - hd))).reshape(d, n_heads * _HP)


def _pack_w_in(w_in):
    ends = np.cumsum(IN_SIZES)
    starts = ends - np.asarray(IN_SIZES)
    qa, ka, va, qi, ki, wi, qg, kg, vg, gk, go = [w_in[:, a:b] for a, b in zip(starts, ends)]
    d = w_in.shape[0]
    pad_to = lambda w: jnp.pad(w, ((0, 0), (0, _HP - w.shape[1])))
    cols = [_pad_heads(qa, N_ATTN_HEADS, ATTN_HEAD_DIM), _pad_heads(ka, N_KV_HEADS, ATTN_HEAD_DIM),
            _pad_heads(qi, N_IDX_HEADS, IDX_HEAD_DIM), _pad_heads(ki, 1, IDX_HEAD_DIM),
            qg, kg, vg, go, pad_to(gk)]
    w = jnp.concatenate(cols, axis=1).astype(BF16)
    assert w.shape == (d, W_IN_PACKED)
    w_t = jnp.concatenate([_pad_heads(va, N_KV_HEADS, ATTN_HEAD_DIM),
                           jnp.pad(wi, ((0, 0), (0, SUBLANE - N_IDX_HEADS)))], axis=1).T.astype(BF16)
    assert w_t.shape == (W_IN_T, d)
    return w, w_t


def _inproj_kernel(x_ref, sc_ref, sh_ref, n1_ref, w_ref, wt_ref, cos_ref, s1_ref, s2_ref, qn_ref, kn_ref,
                   qa_ref, ka_ref, vt_ref, qi_ref, ki_ref, wit_ref,
                   qg_ref, kg_ref, vg_ref, gk_ref, go_ref):
    x = x_ref[...]
    ms = jnp.mean(x * x, axis=-1, keepdims=True)
    h = x * lax.rsqrt(ms + NORM_EPS) * n1_ref[...]
    h = h * (1.0 + sc_ref[0]) + sh_ref[0]
    hb = h.astype(BF16)
    cos = cos_ref[...]
    s1 = s1_ref[...]
    s2 = s2_ref[...]

    def proj(name):
        a, b = _OFF[name]
        return jnp.dot(hb, w_ref[:, a:b], preferred_element_type=F32)

    def rope(v):
        return v * cos + pltpu.roll(v, LANE - 8, 1) * s1 + pltpu.roll(v, 8, 1) * s2

    def head_norm(v, w):
        ssq = jnp.sum(v * v, axis=-1, keepdims=True)
        return v * lax.rsqrt(ssq * (1.0 / ATTN_HEAD_DIM) + NORM_EPS) * w

    p = proj("qa")
    for hh in range(N_ATTN_HEADS):
        sl = slice(hh * _HP, (hh + 1) * _HP)
        qa_ref[:, sl] = (rope(head_norm(p[:, sl], qn_ref[...])) * (ATTN_HEAD_DIM ** -0.5)).astype(BF16)
    p = proj("ka")
    for hh in range(N_KV_HEADS):
        sl = slice(hh * _HP, (hh + 1) * _HP)
        ka_ref[:, sl] = rope(head_norm(p[:, sl], kn_ref[...])).astype(BF16)
    pt = lax.dot_general(wt_ref[...], hb, NT_DIMS, preferred_element_type=F32)
    v_row = lax.broadcasted_iota(I32, (W_KA, pt.shape[1]), 0) % _HP
    vt_ref[0] = jnp.where(v_row == ATTN_HEAD_DIM, 1.0, pt[:W_KA]).astype(BF16)
    wit_ref[...] = pt[W_KA:] * ((N_IDX_HEADS * IDX_HEAD_DIM) ** -0.5)
    p = proj("qi")
    for hh in range(N_IDX_HEADS):
        sl = slice(hh * _HP, (hh + 1) * _HP)
        qi_ref[:, sl] = rope(p[:, sl]).astype(BF16)
    ki_ref[...] = rope(proj("ki")).astype(BF16)
    p = proj("gqk")
    qg_ref[...] = p[:, :W_GQ]
    kg_ref[...] = p[:, W_GQ:]
    vg_ref[...] = proj("gv")
    go_ref[...] = proj("go")
    gk_ref[...] = proj("gk")


def _inproj(x2, sc1, sh1, n1w, w_packed, w_t, cos, s1, s2, qnw, knw, seq):
    t, d = x2.shape
    tm = min(_DSA_KEY_CHUNK, seq)
    per_b = seq // tm
    row = lambda w: pl.BlockSpec((tm, w), lambda i: (i, 0))
    mod = pl.BlockSpec((1, 1, d), lambda i: (i // per_b, 0, 0))
    one = lambda w: pl.BlockSpec((1, w), lambda i: (0, 0))
    sds = jax.ShapeDtypeStruct
    outs = (("qa", row(W_QA), (t, W_QA), BF16), ("ka", row(W_KA), (t, W_KA), BF16),
            ("vt", pl.BlockSpec((1, W_KA, tm), lambda i: (i, 0, 0)), (t // tm, W_KA, tm), BF16),
            ("qi", row(W_QI), (t, W_QI), BF16), ("ki", row(_HP), (t, _HP), BF16),
            ("wit", pl.BlockSpec((SUBLANE, tm), lambda i: (0, i)), (SUBLANE, t), F32),
            ("qg", row(W_GQ), (t, W_GQ), F32), ("kg", row(W_GQ), (t, W_GQ), F32),
            ("vg", row(W_GV), (t, W_GV), F32), ("gk", row(_HP), (t, _HP), F32),
            ("go", row(W_GV), (t, W_GV), F32))
    return pl.pallas_call(
        _inproj_kernel,
        grid=(t // tm,),
        in_specs=[row(d), mod, mod, one(d),
                  pl.BlockSpec((d, W_IN_PACKED), lambda i: (0, 0)),
                  pl.BlockSpec((W_IN_T, d), lambda i: (0, 0)),
                  row(LANE), row(LANE), row(LANE), one(LANE), one(LANE)],
        out_specs=[o[1] for o in outs],
        out_shape=[sds(o[2], o[3]) for o in outs],
        compiler_params=_cparams("arbitrary"),
        name="inproj",
    )(x2, sc1, sh1, n1w, w_packed, w_t, cos, s1, s2, qnw, knw)


_BISECT_STEPS = 8
_DSA_QUERY_TILE = 512
_DSA_SUB_ROWS = 128
_BISECT_MAX_ROUNDS = 96


_FOLD_ROWS = SUBLANE


def _fold_keys(v, op):
    return op(v.reshape(v.shape[0] // _FOLD_ROWS, _FOLD_ROWS, v.shape[1]), axis=0)


def _dsa_kernel(qa_ref, ka_ref, vt_ref, qi_ref, ki_ref, wit_ref, nw_ref, o_ref, sc_ref, jstar_ref,
                *, n_sel, tq, ch, sb, s_len):
    q0 = pl.program_id(1) * tq
    n_chunks = (q0 + tq + ch - 1) // ch
    n_sub = (q0 + tq + sb - 1) // sb
    per_chunk = ch // sb
    wit = wit_ref[...]
    w_rows = [wit[hh:hh + 1, :] for hh in range(N_IDX_HEADS)]
    qcol = q0 + lax.broadcasted_iota(I32, (ch, tq), 1)
    krow = lax.broadcasted_iota(I32, (ch, tq), 0)
    krow_sb = lax.broadcasted_iota(I32, (sb, tq), 0)
    neg = jnp.full((_FOLD_ROWS, tq), -jnp.inf, F32)
    pos = jnp.full((_FOLD_ROWS, tq), jnp.inf, F32)
    zero = jnp.zeros((_FOLD_ROWS, tq), F32)
    col_sum = lambda v: jnp.sum(v, axis=0, keepdims=True)
    col_max = lambda v: jnp.max(v, axis=0, keepdims=True)
    col_min = lambda v: jnp.min(v, axis=0, keepdims=True)

    def score_chunk(c, carry):
        top, bot = carry
        k0 = pl.multiple_of(c * ch, ch)
        kc = ki_ref[pl.ds(k0, ch), :]
        sc = None
        for hh in range(N_IDX_HEADS):
            d = lax.dot_general(kc, qi_ref[:, hh * _HP:(hh + 1) * _HP], NT_DIMS,
                                preferred_element_type=F32)
            term = w_rows[hh] * jnp.maximum(d, 0.0)
            sc = term if sc is None else sc + term
        causal = (k0 + krow) <= qcol
        masked = jnp.where(causal, sc, -jnp.inf)
        for i in range(per_chunk):
            sc_ref[c * per_chunk + i] = masked[i * sb:(i + 1) * sb]
        top = jnp.maximum(top, _fold_keys(jnp.where(causal, sc, -jnp.inf), jnp.max))
        bot = jnp.minimum(bot, _fold_keys(jnp.where(causal, sc, jnp.inf), jnp.min))
        return top, bot

    top, bot = lax.fori_loop(0, n_chunks, score_chunk, (neg, pos))

    n_causal = (q0 + 1 + lax.broadcasted_iota(I32, (1, tq), 1)).astype(F32)
    kk = jnp.minimum(n_causal, float(n_sel))
    lo0 = col_min(bot)
    hi0 = col_max(top)
    hi0 = jnp.where(n_causal <= float(n_sel), lo0, hi0)

    def count_above(piv):
        def body(j, acc):
            return acc + _fold_keys(jnp.where(sc_ref[j] > piv, 1.0, 0.0), jnp.sum)
        return col_sum(lax.fori_loop(0, n_sub, body, zero))

    def search_round(carry):
        lo, hi, _, it = carry
        for _ in range(_BISECT_STEPS):
            mid = lo + 0.5 * (hi - lo)
            stuck = jnp.logical_or(mid <= lo, mid >= hi)
            piv = jnp.where(stuck, lo, mid)
            below = count_above(piv) < kk
            hi_n = jnp.where(below, piv, hi)
            lo = jnp.where(below, lo, jnp.where(stuck, hi, piv))
            hi = hi_n

        def tighten(j, carry):
            t_acc, b_acc = carry
            s = sc_ref[j]
            t_acc = jnp.maximum(t_acc, _fold_keys(jnp.where(s <= hi, s, -jnp.inf), jnp.max))
            b_acc = jnp.minimum(b_acc, _fold_keys(jnp.where(s >= lo, s, jnp.inf), jnp.min))
            return t_acc, b_acc

        t_acc, b_acc = lax.fori_loop(0, n_sub, tighten, (neg, pos))
        top_v = col_max(t_acc)
        bot_v = col_min(b_acc)
        open_cols = jnp.max(jnp.where(top_v > bot_v, 1, 0))
        return bot_v, top_v, open_cols, it + 1

    def search_open(carry):
        return jnp.logical_and(carry[2] > 0, carry[3] < _BISECT_MAX_ROUNDS)

    open0 = jnp.max(jnp.where(hi0 > lo0, 1, 0))
    _, theta, _, _ = lax.while_loop(search_open, search_round, (lo0, hi0, open0, jnp.int32(0)))

    def tally(j, carry):
        a_acc, e_acc = carry
        s = sc_ref[j]
        return (a_acc + _fold_keys(jnp.where(s > theta, 1.0, 0.0), jnp.sum),
                e_acc + _fold_keys(jnp.where(s == theta, 1.0, 0.0), jnp.sum))

    a_acc, e_acc = lax.fori_loop(0, n_sub, tally, (zero, zero))
    need = kk - col_sum(a_acc)
    n_tied = col_sum(e_acc)
    jstar_ref[...] = jnp.full(jstar_ref.shape, s_len, I32)

    @pl.when(jnp.max(jnp.where(n_tied > need, 1, 0)) > 0)
    def _():
        def step(_, bounds):
            lo_i, hi_i = bounds
            mid = (lo_i + hi_i) >> 1

            def body(j, acc):
                hit = jnp.where(sc_ref[j] == theta, jnp.where(j * sb + krow_sb <= mid, 1.0, 0.0), 0.0)
                return acc + _fold_keys(hit, jnp.sum)

            ok = col_sum(lax.fori_loop(0, n_sub, body, zero)) >= need
            return jnp.where(ok, lo_i, mid), jnp.where(ok, mid, hi_i)

        n_steps = int(np.ceil(np.log2(s_len))) + 1
        _, hi_i = lax.fori_loop(0, n_steps, step,
                                (jnp.full((1, tq), -1, I32), jnp.full((1, tq), s_len - 1, I32)))
        jstar_ref[...] = jnp.broadcast_to(hi_i, jstar_ref.shape)

    jstar = jstar_ref[0:1, :]

    def to_bias(j, carry):
        s = sc_ref[j]
        sc_ref[j] = jnp.where(s == theta, jnp.where(j * sb + krow_sb <= jstar, 0.0, -jnp.inf),
                              jnp.where(s > theta, 0.0, -jnp.inf))
        return carry

    lax.fori_loop(0, n_sub, to_bias, 0)

    rep = N_ATTN_HEADS // N_KV_HEADS
    qw = min(LANE, tq)
    kw = min(2 * LANE, ch)
    for g in range(N_KV_HEADS):
        gs = slice(g * _HP, (g + 1) * _HP)

        def att_chunk(c, carry):
            k0 = pl.multiple_of(c * ch, ch)
            kc = ka_ref[pl.ds(k0, ch), gs]
            vt = vt_ref[c, gs, :]
            bias = jnp.concatenate([sc_ref[c * per_chunk + i] for i in range(per_chunk)], axis=0)
            out = []
            for r in range(rep):
                m_all, acc_all = carry[r]
                hh = g * rep + r
                ms, accs = [], []
                for qt in range(tq // qw):
                    ql = slice(qt * qw, (qt + 1) * qw)
                    m, acc = m_all[:, ql], acc_all[:, ql]
                    q_h = qa_ref[ql, hh * _HP:(hh + 1) * _HP]
                    for kt in range(ch // kw):
                        kl = slice(kt * kw, (kt + 1) * kw)
                        s = lax.dot_general(kc[kl], q_h, NT_DIMS, preferred_element_type=F32) + bias[kl, ql]
                        m_new = jnp.maximum(m, col_max(_fold_keys(s, jnp.max)))
                        m_use = jnp.where(m_new == -jnp.inf, 0.0, m_new)
                        p = jnp.exp(s - m_use)
                        acc = jnp.exp(m - m_use) * acc + jnp.dot(vt[:, kl], p.astype(BF16),
                                                                 preferred_element_type=F32)
                        m = m_new
                    ms.append(m)
                    accs.append(acc)
                out.append(tuple(jnp.concatenate(v, axis=1) for v in (ms, accs)))
            return tuple(out)

        init = tuple((jnp.full((1, tq), -jnp.inf, F32), jnp.zeros((_HP, tq), F32)) for _ in range(rep))
        res = lax.fori_loop(0, n_chunks, att_chunk, init)
        dv_row = lax.broadcasted_iota(I32, (_HP, tq), 0)
        for r in range(rep):
            hh = g * rep + r
            acc = res[r][1]
            den = acc[ATTN_HEAD_DIM:ATTN_HEAD_DIM + 1, :]
            o = jnp.where(dv_row < ATTN_HEAD_DIM, acc, 0.0) / den
            ssq = col_sum(o * o)
            on = (o * lax.rsqrt(ssq * (1.0 / ATTN_HEAD_DIM) + NORM_EPS)).T * nw_ref[hh:hh + 1, :]
            o_ref[:, hh * _HP:(hh + 1) * _HP] = on.astype(BF16)


def _dsa(qa, ka, vt, qi, ki, wit, nw, bsz, seq):
    t = qa.shape[0]
    tq = min(_DSA_QUERY_TILE, seq)
    ch = min(_DSA_KEY_CHUNK, seq)
    sb = min(_DSA_SUB_ROWS, ch)
    nq = seq // tq
    nc = seq // ch
    n_sel = min(TOPK_KEYS, seq // 4)
    qrow = lambda w: pl.BlockSpec((tq, w), lambda b, j: (b * nq + j, 0))
    krow = lambda w: pl.BlockSpec((seq, w), lambda b, j: (b, 0))
    return pl.pallas_call(
        functools.partial(_dsa_kernel, n_sel=n_sel, tq=tq, ch=ch, sb=sb, s_len=seq),
        grid=(bsz, nq),
        in_specs=[qrow(W_QA), krow(W_KA), pl.BlockSpec((nc, W_KA, ch), lambda b, j: (b, 0, 0)),
                  qrow(W_QI), krow(_HP), pl.BlockSpec((SUBLANE, tq), lambda b, j: (0, b * nq + j)),
                  pl.BlockSpec((N_ATTN_HEADS, _HP), lambda b, j: (0, 0))],
        out_specs=qrow(W_QA),
        out_shape=jax.ShapeDtypeStruct((t, W_QA), BF16),
        scratch_shapes=[pltpu.VMEM((seq // sb, sb, tq), F32), pltpu.VMEM((SUBLANE, tq), I32)],
        compiler_params=_cparams("arbitrary", "arbitrary"),
        name="dsa",
    )(qa, ka, vt, qi, ki, wit, nw)


_GLA_LEVELS = 6
_GLA_ROWS = 512


def _gla_consts():
    c = GLA_CHUNK
    hk = N_GLA_HEADS * GLA_KEY_DIM
    i = np.arange(c)
    cum = [np.tril(np.ones((c, c)))]
    rq, pm = [], []
    t = i[:, None]
    s = (np.arange(hk) % GLA_KEY_DIM)[None, :]
    pm.append((t == s).astype(np.float32))
    for lvl in range(1, _GLA_LEVELS + 1):
        blk, half = 1 << lvl, 1 << (lvl - 1)
        split = (i // blk) * blk + half - 1
        cum.append((np.arange(c)[None, :] <= split[:, None]).astype(np.float64))
        right = ((i // half) % 2 == 1)
        rq.append(np.broadcast_to(right[:, None], (c, hk)).astype(np.float32))
        pm.append(((t // blk == s // blk) & ((t // half) % 2 == 1) & ((s // half) % 2 == 0))
                  .astype(np.float32))
    r = np.arange(hk)
    bd_k = (r[:, None] // GLA_KEY_DIM == r[None, :] // GLA_KEY_DIM).astype(np.float32)
    cv = np.arange(N_GLA_HEADS * GLA_VAL_DIM)
    bd_v = (r[:, None] // GLA_KEY_DIM == cv[None, :] // GLA_VAL_DIM).astype(np.float32)
    return (jnp.asarray(np.concatenate(cum, 0), BF16), jnp.asarray(np.stack(rq)),
            jnp.asarray(np.stack(pm)), jnp.asarray(bd_k), jnp.asarray(bd_v),
            jnp.asarray(bd_v.T.copy()))


def _gla_kernel(qg_ref, kg_ref, vg_ref, gk_ref, go_ref, wgk_ref, bgk_ref, gnw_ref,
                cum_ref, rq_ref, pm_ref, bdk_ref, bdv_ref, bdvt_ref, o_ref, st_ref):
    c = GLA_CHUNK

    @pl.when(pl.program_id(1) == 0)
    def _():
        st_ref[...] = jnp.zeros(st_ref.shape, F32)

    bdk = bdk_ref[...]
    cum = cum_ref[...]
    for ci in range(_GLA_ROWS // c):
        sl = slice(ci * c, (ci + 1) * c)
        q = qg_ref[sl, :] * (GLA_KEY_DIM ** -0.5)
        k = kg_ref[sl, :]
        v = vg_ref[sl, :]
        z = jnp.dot(gk_ref[sl, :].astype(BF16), wgk_ref[...], preferred_element_type=F32) + bgk_ref[...]
        la = (jnp.minimum(z, 0.0) - jnp.log1p(jnp.exp(-jnp.abs(z)))) * (1.0 / GLA_GATE_TAU)
        la_hi = la.astype(BF16)
        la_lo = (la - la_hi.astype(F32)).astype(BF16)
        cb = (jnp.dot(cum, la_hi, preferred_element_type=F32)
              + jnp.dot(cum, la_lo, preferred_element_type=F32))
        b = cb[0:c]
        kbd = (jnp.concatenate([k] * N_GLA_HEADS, axis=0) * bdk).astype(BF16)
        a = lax.dot_general(q.astype(BF16), kbd, NT_DIMS, preferred_element_type=F32) * pm_ref[0]
        for lvl in range(1, _GLA_LEVELS + 1):
            beta = cb[lvl * c:(lvl + 1) * c]
            rq = rq_ref[lvl - 1]
            qs = q * jnp.exp(jnp.minimum(b - beta, 0.0)) * rq
            ks = k * jnp.exp(jnp.minimum(beta - b, 0.0)) * (1.0 - rq)
            ksbd = (jnp.concatenate([ks] * N_GLA_HEADS, axis=0) * bdk).astype(BF16)
            a = a + lax.dot_general(qs.astype(BF16), ksbd, NT_DIMS,
                                    preferred_element_type=F32) * pm_ref[lvl]
        vbd = (jnp.concatenate([v] * N_GLA_HEADS, axis=0) * bdv_ref[...]).astype(BF16)
        intra = jnp.dot(a.astype(BF16), vbd, preferred_element_type=F32)
        st = st_ref[...]
        inter = lax.dot_general((q * jnp.exp(b)).astype(BF16), st.astype(BF16), NT_DIMS,
                                preferred_element_type=F32)
        o = intra + inter
        b_end = b[c - 1:c, :]
        k_dec = (k * jnp.exp(b_end - b)).astype(BF16)
        upd = lax.dot_general(v.astype(BF16), k_dec, TN_DIMS, preferred_element_type=F32)
        st_ref[...] = st * jnp.exp(b_end) + upd * bdvt_ref[...]
        for hh in range(N_GLA_HEADS):
            hs = slice(hh * GLA_VAL_DIM, (hh + 1) * GLA_VAL_DIM)
            oh = o[:, hs]
            ms = jnp.mean(oh * oh, axis=-1, keepdims=True)
            on = oh * lax.rsqrt(ms + NORM_EPS) * gnw_ref[:, hs]
            o_ref[sl, hs] = (on * _silu(go_ref[sl, hs])).astype(BF16)


def _gla(qg, kg, vg, gk, go, wgk, bgk, gnw, bsz, seq):
    t = qg.shape[0]
    rows = _GLA_ROWS
    assert seq % rows == 0
    nj = seq // rows
    consts = _gla_consts()
    row = lambda w: pl.BlockSpec((rows, w), lambda b, j: (b * nj + j, 0))
    full = lambda a: pl.BlockSpec(a.shape, lambda b, j: (0,) * a.ndim)
    args = (qg, kg, vg, gk, go, wgk, bgk, gnw) + consts
    return pl.pallas_call(
        _gla_kernel,
        grid=(bsz, nj),
        in_specs=[row(W_GQ), row(W_GQ), row(W_GV), row(_HP), row(W_GV)] + [full(a) for a in args[5:]],
        out_specs=row(W_GV),
        out_shape=jax.ShapeDtypeStruct((t, W_GV), BF16),
        scratch_shapes=[pltpu.VMEM((W_GV, W_GQ), F32)],
        compiler_params=_cparams("arbitrary", "arbitrary"),
        name="gla",
    )(*args)


def _outproj_kernel(attn_ref, gla_ref, x_ref, g1_ref, sc_ref, sh_ref, n2_ref, wa_ref, wb_ref,
                    wr_ref, br_ref, tri_ref,
                    x1_ref, h2_ref, idx_ref, wsel_ref, rank_ref, ccol_ref, crow_ref, *, tm):
    @pl.when(pl.program_id(0) == 0)
    def _():
        ccol_ref[...] = jnp.zeros(ccol_ref.shape, F32)
        crow_ref[...] = jnp.zeros(crow_ref.shape, F32)

    mixed = (jnp.dot(attn_ref[...], wa_ref[...], preferred_element_type=F32)
             + jnp.dot(gla_ref[...], wb_ref[...], preferred_element_type=F32))
    x1 = x_ref[...] + g1_ref[0] * mixed
    x1_ref[...] = x1
    ms = jnp.mean(x1 * x1, axis=-1, keepdims=True)
    h2 = x1 * lax.rsqrt(ms + NORM_EPS) * n2_ref[...]
    h2 = h2 * (1.0 + sc_ref[0]) + sh_ref[0]
    _store_slabs(h2_ref, 0, tm, _pack_row_halves(h2))

    logits = lax.dot_general(wr_ref[...], h2.astype(BF16), NT_DIMS, preferred_element_type=F32)
    scores = jax.nn.sigmoid(logits)
    biased = scores + jnp.concatenate([br_ref[...]] * (tm // LANE), axis=1)
    per_group = N_EXPERTS // N_EXPERT_GROUPS
    gs = []
    for g in range(N_EXPERT_GROUPS):
        blk = biased[g * per_group:(g + 1) * per_group]
        m1 = jnp.max(blk, axis=0, keepdims=True)
        n1 = jnp.sum(jnp.where(blk == m1, 1.0, 0.0), axis=0, keepdims=True)
        m2 = jnp.max(jnp.where(blk < m1, blk, -jnp.inf), axis=0, keepdims=True)
        gs.append(m1 + jnp.where(n1 >= 2.0, m1, m2))
    gscore = jnp.concatenate(gs, axis=0)
    gi = lax.broadcasted_iota(I32, gscore.shape, 0)
    gsel = jnp.zeros(gscore.shape, F32)
    for _ in range(TOPK_GROUPS):
        m = jnp.max(gscore, axis=0, keepdims=True)
        first = jnp.min(jnp.where(gscore == m, gi, N_EXPERT_GROUPS), axis=0, keepdims=True)
        hit = gi == first
        gsel = jnp.where(hit, 1.0, gsel)
        gscore = jnp.where(hit, -jnp.inf, gscore)
    gmask = jnp.concatenate(
        [jnp.broadcast_to(gsel[g:g + 1], (per_group, tm)) for g in range(N_EXPERT_GROUPS)], axis=0)
    masked = jnp.where(gmask > 0.0, biased, -jnp.inf)
    ei = lax.broadcasted_iota(I32, masked.shape, 0)
    chosen = jnp.zeros(masked.shape, F32)
    idxs, ws = [], []
    for _ in range(TOPK_EXPERTS):
        m = jnp.max(masked, axis=0, keepdims=True)
        first = jnp.min(jnp.where(masked == m, ei, N_EXPERTS), axis=0, keepdims=True)
        hit = ei == first
        idxs.append(first)
        ws.append(jnp.sum(jnp.where(hit, scores, 0.0), axis=0, keepdims=True))
        chosen = jnp.where(hit, 1.0, chosen)
        masked = jnp.where(hit, -jnp.inf, masked)
    wsum = ws[0]
    for w in ws[1:]:
        wsum = wsum + w
    idx_ref[...] = jnp.concatenate(idxs, axis=0)
    wsel_ref[...] = jnp.concatenate([w / wsum * ROUTED_SCALE for w in ws], axis=0)

    chosen_b = chosen.astype(BF16)
    before = jnp.dot(chosen_b, tri_ref[...], preferred_element_type=F32)
    seen = ccol_ref[...]
    pos = before + jnp.concatenate([seen] * (tm // LANE), axis=1)
    rank_ref[...] = jnp.concatenate(
        [jnp.sum(jnp.where(ei == i, pos, 0.0), axis=0, keepdims=True) for i in idxs],
        axis=0).astype(I32)
    ccol_ref[...] = seen + jnp.sum(chosen, axis=1, keepdims=True)
    crow_ref[...] = crow_ref[...] + lax.dot_general(
        jnp.ones((SUBLANE, tm), BF16), chosen_b, NT_DIMS, preferred_element_type=F32)


def _outproj(attn, gla, x2, g1, sc2, sh2, n2w, wa, wb, wr_t, br, seq):
    t, d = x2.shape
    tm = min(512, seq)
    per_b = seq // tm
    row = lambda w: pl.BlockSpec((tm, w), lambda i: (i, 0))
    col = lambda: pl.BlockSpec((TOPK_EXPERTS, tm), lambda i: (0, i))
    mod = pl.BlockSpec((1, 1, d), lambda i: (i // per_b, 0, 0))
    full = lambda a: pl.BlockSpec(a.shape, lambda i: (0,) * a.ndim)
    tri = jnp.asarray(np.triu(np.ones((tm, tm), np.float32), 1), BF16)
    br_col = jnp.broadcast_to(br.reshape(N_EXPERTS, 1).astype(F32), (N_EXPERTS, LANE))
    n2 = n2w.reshape(1, d)
    n_slabs = d // 2 // LANE
    sel = jax.ShapeDtypeStruct((TOPK_EXPERTS, t), I32)
    return pl.pallas_call(
        functools.partial(_outproj_kernel, tm=tm),
        grid=(t // tm,),
        in_specs=[row(W_QA), row(W_GV), row(d), mod, mod, mod, full(n2), full(wa), full(wb),
                  full(wr_t), full(br_col), full(tri)],
        out_specs=[row(d), pl.BlockSpec((tm * n_slabs, LANE), lambda i: (i, 0)), col(), col(), col(),
                   pl.BlockSpec((N_EXPERTS, LANE), lambda i: (0, 0)),
                   pl.BlockSpec((SUBLANE, N_EXPERTS), lambda i: (0, 0))],
        out_shape=[jax.ShapeDtypeStruct((t, d), F32), jax.ShapeDtypeStruct((t * n_slabs, LANE), U32),
                   sel, jax.ShapeDtypeStruct((TOPK_EXPERTS, t), F32), sel,
                   jax.ShapeDtypeStruct((N_EXPERTS, LANE), F32),
                   jax.ShapeDtypeStruct((SUBLANE, N_EXPERTS), F32)],
        compiler_params=_cparams("arbitrary"),
        name="outproj",
    )(attn, gla, x2, g1, sc2, sh2, n2, wa, wb, wr_t, br_col, tri)


def _plan_kernel(ccol_ref, crow_ref, idx_ref, rank_ref, dest_ref, blk_ref, nblk_ref, *, n_blocks):
    m = float(EXPERT_ROWS)
    pad_row = jnp.floor((crow_ref[0:1, :] + (m - 1.0)) * (1.0 / m)) * m
    pad_col = jnp.floor((ccol_ref[:, 0:1] + (m - 1.0)) * (1.0 / m)) * m
    er = lax.broadcasted_iota(I32, (N_EXPERTS, N_EXPERTS), 0)
    ec = lax.broadcasted_iota(I32, (N_EXPERTS, N_EXPERTS), 1)
    start_col = jnp.sum(jnp.where(ec < er, pad_row, 0.0), axis=1, keepdims=True)
    end_col = start_col + pad_col
    first_row = (lax.broadcasted_iota(I32, (N_EXPERTS, n_blocks), 1) * EXPERT_ROWS).astype(F32)
    owner = jnp.sum(jnp.where(end_col <= first_row, 1.0, 0.0), axis=0, keepdims=True)
    blk_ref[...] = jnp.minimum(owner, float(N_EXPERTS - 1)).astype(I32)
    total = jnp.sum(pad_row, axis=1, keepdims=True)
    nblk_ref[...] = jnp.broadcast_to(total * (1.0 / m), nblk_ref.shape).astype(I32)
    tm = idx_ref.shape[1]
    ei = lax.broadcasted_iota(I32, (N_EXPERTS, tm), 0)
    rows = []
    for kc in range(TOPK_EXPERTS):
        hit = ei == idx_ref[kc:kc + 1, :]
        rows.append(jnp.sum(jnp.where(hit, start_col, 0.0), axis=0, keepdims=True))
    dest_ref[...] = jnp.concatenate(rows, axis=0).astype(I32) + rank_ref[...]


def _plan(ccol, crow, idx, rank, n_blocks):
    t = idx.shape[1]
    tm = min(1024, t)
    col = pl.BlockSpec((TOPK_EXPERTS, tm), lambda i: (0, i))
    full = lambda a: pl.BlockSpec(a.shape, lambda i: (0,) * a.ndim)
    return pl.pallas_call(
        functools.partial(_plan_kernel, n_blocks=n_blocks),
        grid=(t // tm,),
        in_specs=[full(ccol), full(crow), col, col],
        out_specs=[col, pl.BlockSpec((1, n_blocks), lambda i: (0, 0)),
                   pl.BlockSpec((1, LANE), lambda i: (0, 0))],
        out_shape=[jax.ShapeDtypeStruct((TOPK_EXPERTS, t), I32),
                   jax.ShapeDtypeStruct((1, n_blocks), I32),
                   jax.ShapeDtypeStruct((1, LANE), I32)],
        compiler_params=_cparams("arbitrary"),
        name="plan",
    )(ccol, crow, idx, rank)


def _row_copy(src, src_row, dst, dst_row, sem, n_slabs):
    return pltpu.make_async_copy(src.at[pl.ds(src_row * n_slabs, n_slabs), :],
                                 dst.at[pl.ds(dst_row * n_slabs, n_slabs), :], sem)


def _scatter_kernel(dest_ref, h_ref, xb_in_ref, xb_ref, sem, *, tm, n_slabs):
    del xb_in_ref

    def issue(tok, carry):
        for kc in range(TOPK_EXPERTS):
            _row_copy(h_ref, tok, xb_ref, dest_ref[kc, tok], sem, n_slabs).start()
        return carry

    lax.fori_loop(0, tm, issue, 0)

    def drain(tok, carry):
        for kc in range(TOPK_EXPERTS):
            _row_copy(h_ref, tok, xb_ref, dest_ref[kc, tok], sem, n_slabs).wait()
        return carry

    lax.fori_loop(0, tm, drain, 0)


def _scatter(dest, h_rows, xb_zero, n_slabs):
    t = dest.shape[1]
    tm = min(512, t)
    return pl.pallas_call(
        functools.partial(_scatter_kernel, tm=tm, n_slabs=n_slabs),
        grid=(t // tm,),
        in_specs=[pl.BlockSpec((TOPK_EXPERTS, tm), lambda i: (0, i), memory_space=pltpu.SMEM),
                  pl.BlockSpec((tm * n_slabs, LANE), lambda i: (i, 0)),
                  pl.BlockSpec(memory_space=pl.ANY)],
        out_specs=pl.BlockSpec(memory_space=pl.ANY),
        out_shape=jax.ShapeDtypeStruct(xb_zero.shape, U32),
        scratch_shapes=[pltpu.SemaphoreType.DMA],
        input_output_aliases={2: 0},
        compiler_params=_cparams("arbitrary"),
        name="scatter",
    )(dest, h_rows, xb_zero)


def _experts_kernel(blk_ref, nblk_ref, x_ref, wg_ref, wu_ref, wd_ref, y_ref, wgb, wub, wdb, *, n_slabs):
    i = pl.program_id(0)
    m = EXPERT_ROWS

    @pl.when(i >= nblk_ref[0])
    def _():
        y_ref[...] = jnp.zeros(y_ref.shape, U32)

    @pl.when(i < nblk_ref[0])
    def _():
        prev = blk_ref[jnp.maximum(i - 1, 0)]

        @pl.when(jnp.logical_or(i == 0, blk_ref[i] != prev))
        def _():
            wgb[...] = wg_ref[0].astype(BF16)
            wub[...] = wu_ref[0].astype(BF16)
            wdb[...] = wd_ref[0].astype(BF16)

        x = _unpack_row_halves(_load_slabs(x_ref, 0, m, n_slabs)).astype(BF16)
        g = jnp.dot(x, wgb[...], preferred_element_type=F32)
        u = jnp.dot(x, wub[...], preferred_element_type=F32)
        y = jnp.dot((_silu(g) * u).astype(BF16), wdb[...], preferred_element_type=F32)
        _store_slabs(y_ref, 0, m, _pack_row_halves(y))


def _experts(blk, nblk, xb, we_gate, we_up, we_down, n_blocks):
    m = EXPERT_ROWS
    e, d, f = we_gate.shape
    n_slabs = d // 2 // LANE
    live = lambda i, blk, nblk: jnp.minimum(i, jnp.maximum(nblk[0] - 1, 0))
    grid_spec = pltpu.PrefetchScalarGridSpec(
        num_scalar_prefetch=2,
        grid=(n_blocks,),
        in_specs=[pl.BlockSpec((m * n_slabs, LANE), lambda i, blk, nblk: (live(i, blk, nblk), 0)),
                  pl.BlockSpec((1, d, f), lambda i, blk, nblk: (blk[live(i, blk, nblk)], 0, 0)),
                  pl.BlockSpec((1, d, f), lambda i, blk, nblk: (blk[live(i, blk, nblk)], 0, 0)),
                  pl.BlockSpec((1, f, d), lambda i, blk, nblk: (blk[live(i, blk, nblk)], 0, 0))],
        out_specs=pl.BlockSpec((m * n_slabs, LANE), lambda i, blk, nblk: (i, 0)),
        scratch_shapes=[pltpu.VMEM((d, f), BF16), pltpu.VMEM((d, f), BF16), pltpu.VMEM((f, d), BF16)],
    )
    return pl.pallas_call(
        functools.partial(_experts_kernel, n_slabs=n_slabs),
        grid_spec=grid_spec,
        out_shape=jax.ShapeDtypeStruct(xb.shape, U32),
        compiler_params=_cparams("arbitrary"),
        name="experts",
    )(blk, nblk, xb, we_gate, we_up, we_down)


def _combine_kernel(dest_ref, yb_ref, w_ref, h_ref, x1_ref, g2_ref, wsg_ref, wsu_ref, wsd_ref,
                    o_ref, buf, sems, *, tm, n_parts, n_slabs):
    tp = tm // n_parts

    def copies(part, tok, kc):
        return _row_copy(yb_ref, dest_ref[kc, tok], buf, kc * tm + tok, sems.at[part], n_slabs)

    for part in range(n_parts):
        def issue(tok, carry, part=part):
            for kc in range(TOPK_EXPERTS):
                copies(part, tok, kc).start()
            return carry

        lax.fori_loop(part * tp, (part + 1) * tp, issue, 0)

    hb = _unpack_row_halves(_load_slabs(h_ref, 0, tm, n_slabs)).astype(BF16)
    g = jnp.dot(hb, wsg_ref[...], preferred_element_type=F32)
    u = jnp.dot(hb, wsu_ref[...], preferred_element_type=F32)
    shared = jnp.dot((_silu(g) * u).astype(BF16), wsd_ref[...], preferred_element_type=F32)

    for part in range(n_parts):
        def drain(tok, carry, part=part):
            for kc in range(TOPK_EXPERTS):
                copies(part, tok, kc).wait()
            return carry

        rows = slice(part * tp, (part + 1) * tp)
        lax.fori_loop(part * tp, (part + 1) * tp, drain, 0)
        w = w_ref[rows, :]
        routed = None
        for kc in range(TOPK_EXPERTS):
            term = w[:, kc:kc + 1] * _unpack_row_halves(_load_slabs(buf, kc * tm + part * tp, tp, n_slabs))
            routed = term if routed is None else routed + term
        o_ref[rows, :] = x1_ref[rows, :] + g2_ref[0] * (routed + shared[rows])


def _combine(dest, yb, w_tok, h_rows, x1, g2, wsg, wsu, wsd, seq):
    t, d = x1.shape
    tm = min(256, seq)
    per_b = seq // tm
    n_slabs = d // 2 // LANE
    n_parts = 4
    row = lambda w: pl.BlockSpec((tm, w), lambda i: (i, 0))
    full = lambda a: pl.BlockSpec(a.shape, lambda i: (0,) * a.ndim)
    return pl.pallas_call(
        functools.partial(_combine_kernel, tm=tm, n_parts=n_parts, n_slabs=n_slabs),
        grid=(t // tm,),
        in_specs=[pl.BlockSpec((TOPK_EXPERTS, tm), lambda i: (0, i), memory_space=pltpu.SMEM),
                  pl.BlockSpec(memory_space=pl.ANY),
                  row(TOPK_EXPERTS), pl.BlockSpec((tm * n_slabs, LANE), lambda i: (i, 0)), row(d),
                  pl.BlockSpec((1, 1, d), lambda i: (i // per_b, 0, 0)),
                  full(wsg), full(wsu), full(wsd)],
        out_specs=row(d),
        out_shape=jax.ShapeDtypeStruct((t, d), F32),
        scratch_shapes=[pltpu.VMEM((TOPK_EXPERTS * tm * n_slabs, LANE), U32),
                        pltpu.SemaphoreType.DMA((n_parts,))],
        compiler_params=_cparams("arbitrary"),
        name="combine",
    )(dest, yb, w_tok, h_rows, x1, g2, wsg, wsu, wsd)


def _layer(x2, cond_mod, positions_tables, p, bsz, seq):
    t, d = x2.shape
    cos, s1, s2 = positions_tables
    sh1, sc1, g1, sh2, sc2, g2 = [m.reshape(bsz, 1, d) for m in jnp.split(cond_mod, 6, axis=-1)]

    lane_pad = lambda w: jnp.pad(w.astype(F32), ((0, 0), (0, _HP - w.shape[-1])))
    qnw = lane_pad(p["q_norm_w"].reshape(1, ATTN_HEAD_DIM))
    knw = lane_pad(p["k_norm_w"].reshape(1, ATTN_HEAD_DIM))
    qa, ka, vt, qi, ki, wit, qg, kg, vg, gk, go = _inproj(
        x2, sc1, sh1, p["norm1_w"].reshape(1, d), *_pack_w_in(p["w_in"]), cos, s1, s2, qnw, knw, seq)

    attn = _dsa(qa, ka, vt, qi, ki, wit, lane_pad(p["attn_out_norm_w"]), bsz, seq)

    wgk = jnp.pad(p["w_gk2"], ((0, _HP - GLA_GATE_RANK), (0, 0))).astype(BF16)
    gla = _gla(qg, kg, vg, gk, go, wgk, p["b_gk"].reshape(1, W_GQ).astype(F32),
               p["gla_norm_w"].reshape(1, W_GV).astype(F32), bsz, seq)

    w_out = p["w_out"]
    n_attn = N_ATTN_HEADS * ATTN_HEAD_DIM
    wa = jnp.pad(w_out[:n_attn].reshape(N_ATTN_HEADS, ATTN_HEAD_DIM, d),
                 ((0, 0), (0, _HP - ATTN_HEAD_DIM), (0, 0))).reshape(W_QA, d).astype(BF16)
    wb = w_out[n_attn:].astype(BF16)
    x1, h2, idx, wsel, rank, ccol, crow = _outproj(
        attn, gla, x2, g1, sc2, sh2, p["norm2_w"], wa, wb,
        p["w_router"].T.astype(BF16), p["b_router"], seq)

    n_blocks = (t * TOPK_EXPERTS) // EXPERT_ROWS + N_EXPERTS
    dest, blk, nblk = _plan(ccol, crow, idx, rank, n_blocks)

    n_rows = n_blocks * EXPERT_ROWS
    n_slabs = d // 2 // LANE
    xb = _scatter(dest, h2, jnp.zeros((n_rows * n_slabs, LANE), U32), n_slabs)
    yb = _experts(blk.reshape(n_blocks), nblk.reshape(LANE)[:1], xb,
                  p["we_gate"], p["we_up"], p["we_down"], n_blocks)
    return _combine(dest, yb, wsel.T, h2, x1, g2, p["ws_gate"].astype(BF16),
                    p["ws_up"].astype(BF16), p["ws_down"].astype(BF16), seq)


def kernel(x, c, positions, norm1_w, norm2_w, w_ada, b_ada, w_in, q_norm_w, k_norm_w,
           attn_out_norm_w, w_gk2, b_gk, gla_norm_w, w_out, w_router, b_router,
           we_gate, we_up, we_down, ws_gate, ws_up, ws_down):
    bsz, seq, d = x.shape
    depth = w_ada.shape[0]
    x2 = x.reshape(bsz * seq, d)
    tables = _rope_tables(positions)
    stacked = dict(norm1_w=norm1_w, norm2_w=norm2_w, w_in=w_in, q_norm_w=q_norm_w, k_norm_w=k_norm_w,
                   attn_out_norm_w=attn_out_norm_w, w_gk2=w_gk2, b_gk=b_gk, gla_norm_w=gla_norm_w,
                   w_out=w_out, w_router=w_router, b_router=b_router, we_gate=we_gate, we_up=we_up,
                   we_down=we_down, ws_gate=ws_gate, ws_up=ws_up, ws_down=ws_down)
    for l in range(depth):
        mod = _ada(c.astype(F32), w_ada[l], b_ada[l])
        x2 = _layer(x2, mod, tables, {k: v[l] for k, v in stacked.items()}, bsz, seq)
    return x2.reshape(bsz, seq, d)
```
